```python
import math
import jax, jax.numpy as jnp
from jax import lax
import numpy as np

D_MODEL = 4096
BATCH = 16
SEQ = 256
DEPTH = 4
DEC_BATCH = 8
DEC_SEQ = 2048
PAST_LEN = 512

GRID_W = 64
N_MIXERS = 2
NA_HEADS = 32
NA_HEAD_DIM = D_MODEL // NA_HEADS
WIN_R = 8
WIN_C = 16
KEY_BLOCK_C = 2 * WIN_C
N_COL_BLOCKS = GRID_W // WIN_C
DIFF_HEADS = 16
DIFF_QK_DIM = D_MODEL // (2 * DIFF_HEADS)
DIFF_V_DIM = 2 * DIFF_QK_DIM
D_FF = 11008
CONV_W = 3
ROPE_THETA = 10000.0
NORM_EPS = 1e-6
DIFF_NORM_EPS = 1e-5
Q_BLOCK = 128
NEG_INF = -1e30
N_NA_LAYERS = (DEPTH + 1) // 2
N_DIFF_LAYERS = DEPTH // 2

kernel_name = "hybrid_na_diffattn_convffn_diffusion_step"


def rms_norm(x, g, eps=NORM_EPS):
    x32 = x.astype(jnp.float32)
    y = x32 * lax.rsqrt(jnp.mean(x32 * x32, axis=-1, keepdims=True) + eps)
    return (y * g.astype(jnp.float32)).astype(x.dtype)


def modulate(h, shift, scale):
    return h * (1 + scale[:, None, :]) + shift[:, None, :]


def ada_params(cond, w_ada_l, b_ada_l):
    mod = jax.nn.silu(cond) @ w_ada_l + b_ada_l
    return jnp.split(mod, 6, axis=-1)


def cache_shapes(batch, length, l):
    if l % N_MIXERS == 0:
        return (batch, NA_HEADS, length, NA_HEAD_DIM), (batch, NA_HEADS, length, NA_HEAD_DIM)
    return (batch, DIFF_HEADS, length, 2, DIFF_QK_DIM), (batch, DIFF_HEADS, length, DIFF_V_DIM)


def to_heads(t, n_heads):
    B, S, _ = t.shape
    return t.reshape(B, S, n_heads, -1).transpose(0, 2, 1, 3)


def to_diff_qk(t):
    B, S, _ = t.shape
    return t.reshape(B, S, DIFF_HEADS, 2, DIFF_QK_DIM).transpose(0, 2, 1, 3, 4)


def merge_heads(o):
    B, H, S, d = o.shape
    return o.transpose(0, 2, 1, 3).reshape(B, S, H * d)


def sweep_query_blocks(fn, q):
    B, H, S = q.shape[:3]
    nb = S // Q_BLOCK
    qb = jnp.moveaxis(q.reshape((B, H, nb, Q_BLOCK) + q.shape[3:]), 2, 0)
    out = lax.map(fn, qb)
    out = jnp.moveaxis(out, 0, 2)
    return out.reshape((B, H, S) + out.shape[4:])


def dense_attention(q, k, v):
    scale = q.shape[-1] ** -0.5

    def blk(qb):
        s = jnp.einsum('bhqd,bhkd->bhqk', qb, k).astype(jnp.float32) * scale
        p = jax.nn.softmax(s, axis=-1).astype(v.dtype)
        return jnp.einsum('bhqk,bhkd->bhqd', p, v)

    return sweep_query_blocks(blk, q)


def diff_attention(q, k, v, lam):
    scale = q.shape[-1] ** -0.5

    def blk(qb):
        s = jnp.einsum('bhqcd,bhkcd->bhcqk', qb, k).astype(jnp.float32) * scale
        p = jax.nn.softmax(s, axis=-1)
        w = (p[:, :, 0] - lam * p[:, :, 1]).astype(v.dtype)
        return jnp.einsum('bhqk,bhkd->bhqd', w, v)

    return sweep_query_blocks(blk, q)


def diff_lambda_value(lam_params, l):
    lam_init = 0.8 - 0.6 * math.exp(-0.3 * l)
    p = lam_params.astype(jnp.float32)
    lam = jnp.exp(jnp.sum(p[0] * p[1])) - jnp.exp(jnp.sum(p[2] * p[3])) + lam_init
    return lam, lam_init


def diff_head_norm(o, g, lam_init):
    o32 = o.astype(jnp.float32)
    y = o32 * lax.rsqrt(jnp.mean(o32 * o32, axis=-1, keepdims=True) + DIFF_NORM_EPS)
    return (y * g.astype(jnp.float32) * (1.0 - lam_init)).astype(o.dtype)


def rope_axial(x, pos_r, pos_c):
    half = x.shape[-1] // 2
    nf = half // 2
    inv = ROPE_THETA ** (-jnp.arange(nf, dtype=jnp.float32) / nf)

    def rot(xa, pos):
        ang = pos[:, None] * inv[None, :]
        cos = jnp.cos(ang)[:, None, :].astype(x.dtype)
        sin = jnp.sin(ang)[:, None, :].astype(x.dtype)
        x1, x2 = xa[..., :nf], xa[..., nf:]
        return jnp.concatenate([x1 * cos - x2 * sin, x2 * cos + x1 * sin], axis=-1)

    return jnp.concatenate([rot(x[..., :half], pos_r), rot(x[..., half:], pos_c)], axis=-1)


def na_column_tables():
    j = np.arange(N_COL_BLOCKS)
    kcs = np.clip(j * WIN_C - WIN_C // 2, 0, GRID_W - KEY_BLOCK_C)
    key_cols = kcs[:, None] + np.arange(KEY_BLOCK_C)
    q_cols = j[:, None] * WIN_C + np.arange(WIN_C)
    cs = np.clip(q_cols - WIN_C // 2, 0, GRID_W - WIN_C)
    kc = key_cols[:, None, :]
    in_win = (kc >= cs[:, :, None]) & (kc < cs[:, :, None] + WIN_C)
    dc_idx = np.clip(kc - q_cols[:, :, None], -(WIN_C - 1), WIN_C - 1) + WIN_C - 1
    return key_cols, in_win, dc_idx


def neighbourhood_attention(q, k, v, k_ctx, v_ctx, rpb):
    B, H, N, dh = q.shape
    rows = N // GRID_W
    wr = min(WIN_R, rows)
    n_loc = wr * KEY_BLOCK_C
    scale = dh ** -0.5
    key_cols, in_win, dc_idx = na_column_tables()
    mask = np.broadcast_to(in_win[:, :, None, :], (N_COL_BLOCKS, WIN_C, wr, KEY_BLOCK_C)).reshape(
        N_COL_BLOCKS, WIN_C, n_loc)
    kg = k.reshape(B, H, rows, GRID_W, dh)
    vg = v.reshape(B, H, rows, GRID_W, dh)
    qg = jnp.moveaxis(q.reshape(B, H, rows, N_COL_BLOCKS, WIN_C, dh), 2, 0)

    def row_block(args):
        r, qr = args
        rs = jnp.clip(r - wr // 2, 0, rows - wr)

        def gather(t):
            t = lax.dynamic_slice_in_dim(t, rs, wr, axis=2)
            t = t[:, :, :, key_cols, :]
            return jnp.moveaxis(t, 3, 2).reshape(B, H, N_COL_BLOCKS, n_loc, dh)

        kb, vb = gather(kg), gather(vg)
        dr_idx = rs + jnp.arange(wr) - r + WIN_R - 1
        bias = rpb[:, dr_idx[None, None, :, None], dc_idx[:, :, None, :]]
        bias = bias.reshape(H, N_COL_BLOCKS, WIN_C, n_loc).astype(jnp.float32)
        s_loc = jnp.einsum('bhjqd,bhjkd->bhjqk', qr, kb).astype(jnp.float32) * scale + bias
        s_loc = jnp.where(mask, s_loc, NEG_INF)
        s_ctx = jnp.einsum('bhjqd,bhld->bhjql', qr, k_ctx).astype(jnp.float32) * scale
        p = jax.nn.softmax(jnp.concatenate([s_loc, s_ctx], axis=-1), axis=-1).astype(v.dtype)
        o = (jnp.einsum('bhjqk,bhjkd->bhjqd', p[..., :n_loc], vb)
             + jnp.einsum('bhjql,bhld->bhjqd', p[..., n_loc:], v_ctx))
        return o.reshape(B, H, GRID_W, dh)

    out = lax.map(row_block, (jnp.arange(rows), qg))
    return jnp.moveaxis(out, 0, 2).reshape(B, H, N, dh)


def conv_ffn(h, w_up_l, conv_w_l, conv_b_l, w_down_l):
    S = h.shape[1]
    u = h @ w_up_l
    pad = CONV_W // 2
    up = jnp.pad(u, ((0, 0), (pad, pad), (0, 0)))
    u = sum(up[:, i:i + S] * conv_w_l[i] for i in range(CONV_W)) + conv_b_l
    a, b = jnp.split(u, 2, axis=-1)
    return (jax.nn.silu(a) * b) @ w_down_l


def context_mixer(h, l, w_qkv_l, diff_lambda, diff_subln):
    q, k, v = jnp.split(h @ w_qkv_l, 3, axis=-1)
    if l % N_MIXERS == 0:
        q, k, v = to_heads(q, NA_HEADS), to_heads(k, NA_HEADS), to_heads(v, NA_HEADS)
        o = dense_attention(q, k, v)
    else:
        q, k, v = to_diff_qk(q), to_diff_qk(k), to_heads(v, DIFF_HEADS)
        lam, lam_init = diff_lambda_value(diff_lambda[l // N_MIXERS], l)
        o = diff_head_norm(diff_attention(q, k, v, lam), diff_subln[l // N_MIXERS], lam_init)
    return merge_heads(o), k, v


def latent_mixer(h, l, k_ctx, v_ctx, pos_r, pos_c, w_qkv_l, na_rpb, diff_lambda, diff_subln):
    q, k, v = jnp.split(h @ w_qkv_l, 3, axis=-1)
    if l % N_MIXERS == 0:
        q, k, v = to_heads(q, NA_HEADS), to_heads(k, NA_HEADS), to_heads(v, NA_HEADS)
        o = neighbourhood_attention(q, k, v, k_ctx, v_ctx, na_rpb[l // N_MIXERS])
    else:
        q = rope_axial(to_diff_qk(q), pos_r, pos_c)
        k = rope_axial(to_diff_qk(k), pos_r, pos_c)
        v = to_heads(v, DIFF_HEADS)
        keys = jnp.concatenate([k, k_ctx.astype(k.dtype)], axis=2)
        vals = jnp.concatenate([v, v_ctx.astype(v.dtype)], axis=2)
        lam, lam_init = diff_lambda_value(diff_lambda[l // N_MIXERS], l)
        o = diff_head_norm(diff_attention(q, keys, vals, lam), diff_subln[l // N_MIXERS], lam_init)
    return merge_heads(o)


def setup_inputs(seed: int = 0) -> dict:
    key = jax.random.key(seed)
    ks = list(jax.random.split(key, 32))
    counter = [0]

    def nrm(shape, s):
        kk = ks[counter[0]]
        counter[0] += 1
        return jax.random.normal(kk, shape, jnp.float32) * s

    inputs = {}
    inputs['x_prompt'] = nrm((BATCH, SEQ, D_MODEL), 1.0)
    inputs['x_sample'] = nrm((DEC_BATCH, DEC_SEQ, D_MODEL), 1.0)
    for l in range(DEPTH):
        k_shape, v_shape = cache_shapes(DEC_BATCH, PAST_LEN, l)
        inputs['cache_k_%d' % l] = nrm(k_shape, 1.0)
        inputs['cache_v_%d' % l] = nrm(v_shape, 1.0)
    inputs['c'] = nrm((DEC_BATCH, D_MODEL), 1.0)
    inputs['c_ctx'] = nrm((D_MODEL,), 1.0)
    inputs['norm_attn'] = 1.0 + nrm((DEPTH, D_MODEL), 0.01)
    inputs['norm_ffn'] = 1.0 + nrm((DEPTH, D_MODEL), 0.01)
    inputs['w_ada'] = nrm((DEPTH, D_MODEL, 6 * D_MODEL), 0.5 * D_MODEL ** -0.5)
    inputs['b_ada'] = nrm((DEPTH, 6 * D_MODEL), 0.01)
    inputs['w_qkv'] = nrm((DEPTH, D_MODEL, 3 * D_MODEL), D_MODEL ** -0.5)
    inputs['w_o'] = nrm((DEPTH, D_MODEL, D_MODEL), D_MODEL ** -0.5)
    inputs['na_rpb'] = nrm((N_NA_LAYERS, NA_HEADS, 2 * WIN_R - 1, 2 * WIN_C - 1), 0.02)
    inputs['diff_lambda'] = nrm((N_DIFF_LAYERS, 4, DIFF_QK_DIM), 0.1)
    inputs['diff_subln'] = 1.0 + nrm((N_DIFF_LAYERS, DIFF_V_DIM), 0.01)
    inputs['w_up'] = nrm((DEPTH, D_MODEL, 2 * D_FF), D_MODEL ** -0.5)
    inputs['conv_w'] = nrm((DEPTH, CONV_W, 2 * D_FF), CONV_W ** -0.5)
    inputs['conv_b'] = nrm((DEPTH, 2 * D_FF), 0.01)
    inputs['w_down'] = nrm((DEPTH, D_FF, D_MODEL), D_FF ** -0.5)
    inputs['final_norm'] = 1.0 + nrm((D_MODEL,), 0.01)
    return inputs


def reference(x_prompt, x_sample, cache_k_0, cache_v_0, cache_k_1, cache_v_1, cache_k_2, cache_v_2,
              cache_k_3, cache_v_3, c, c_ctx, norm_attn, norm_ffn, w_ada, b_ada, w_qkv, w_o, na_rpb,
              diff_lambda, diff_subln, w_up, conv_w, conv_b, w_down, final_norm):
    caches = [(cache_k_0, cache_v_0), (cache_k_1, cache_v_1), (cache_k_2, cache_v_2), (cache_k_3, cache_v_3)]
    n_lat = x_sample.shape[1]
    t = jnp.arange(n_lat)
    pos_r = (t // GRID_W).astype(jnp.float32)
    pos_c = (t % GRID_W).astype(jnp.float32)
    cond_ctx = c_ctx[None, :]

    xp, xs = x_prompt, x_sample
    new_state = []
    for l in range(DEPTH):
        sh_a, sc_a, g_a, sh_f, sc_f, g_f = ada_params(cond_ctx, w_ada[l], b_ada[l])
        h = modulate(rms_norm(xp, norm_attn[l]), sh_a, sc_a)
        o, k_new, v_new = context_mixer(h, l, w_qkv[l], diff_lambda, diff_subln)
        new_state.append(k_new)
        new_state.append(v_new)
        xp = xp + g_a[:, None, :] * (o @ w_o[l])
        h = modulate(rms_norm(xp, norm_ffn[l]), sh_f, sc_f)
        xp = xp + g_f[:, None, :] * conv_ffn(h, w_up[l], conv_w[l], conv_b[l], w_down[l])

        k_ctx, v_ctx = caches[l]
        sh_a, sc_a, g_a, sh_f, sc_f, g_f = ada_params(c, w_ada[l], b_ada[l])
        h = modulate(rms_norm(xs, norm_attn[l]), sh_a, sc_a)
        o = latent_mixer(h, l, k_ctx, v_ctx, pos_r, pos_c, w_qkv[l], na_rpb, diff_lambda, diff_subln)
        xs = xs + g_a[:, None, :] * (o @ w_o[l])
        h = modulate(rms_norm(xs, norm_ffn[l]), sh_f, sc_f)
        xs = xs + g_f[:, None, :] * conv_ffn(h, w_up[l], conv_w[l], conv_b[l], w_down[l])

    y_prompt = rms_norm(xp, final_norm)
    y_sample = rms_norm(xs, final_norm)
    return (y_prompt, y_sample, *new_state)
```

```python
import functools
import math

import numpy as np
import jax
import jax.numpy as jnp
from jax import lax
from jax.experimental import pallas as pl
from jax.experimental.pallas import tpu as pltpu

GRID_W = 64
ROPE_THETA = 10000.0
NORM_EPS = 1e-6
DIFF_NORM_EPS = 1e-5
NEG_INF = -1e30
NA_GROUP_ROWS = 4
V7X_VMEM_BYTES = 64 * 1024 * 1024
V7X_VMEM_RESERVE = 6 * 1024 * 1024
F32 = jnp.float32
BF16 = jnp.bfloat16


def _cparams(semantics, need_bytes):
    limit = int(min(V7X_VMEM_BYTES - V7X_VMEM_RESERVE, max(need_bytes, 16 * 1024 * 1024)))
    return pltpu.CompilerParams(dimension_semantics=semantics, vmem_limit_bytes=limit)


def _tile(n, want, quantum=128):
    if n <= want:
        return n
    t = (want // quantum) * quantum
    while t >= quantum:
        if n % t == 0:
            return t
        t -= quantum
    return n


def _dot(a, b):
    return jnp.dot(a, b, preferred_element_type=F32)


def _dot_nt(a, b):
    return lax.dot_general(a, b, (((1,), (1,)), ((), ())), preferred_element_type=F32)


def _ada_kernel(c_ref, w_ref, b_ref, o_ref):
    c = c_ref[...]
    a = (c * jax.nn.sigmoid(c)).astype(BF16)
    o_ref[0] = _dot(a, w_ref[0].astype(BF16)) + b_ref[0]


def _ada_params(cond, w_ada, b_ada):
    L, D, N = w_ada.shape
    R = cond.shape[0]
    tn = _tile(N, 512)
    need = 2 * (D * tn * 4 + R * tn * 8) + 2 * R * D * 4 + D * tn * 2
    return pl.pallas_call(
        _ada_kernel,
        grid=(L, N // tn),
        in_specs=[pl.BlockSpec((R, D), lambda l, j: (0, 0)),
                  pl.BlockSpec((1, D, tn), lambda l, j: (l, 0, j)),
                  pl.BlockSpec((1, 1, tn), lambda l, j: (l, 0, j))],
        out_specs=pl.BlockSpec((1, R, tn), lambda l, j: (l, 0, j)),
        out_shape=jax.ShapeDtypeStruct((L, R, N), F32),
        compiler_params=_cparams(("parallel", "parallel"), need),
        name="ada_params",
    )(cond, w_ada, b_ada.reshape(L, 1, N))


def _norm_kernel(x_ref, g_ref, *rest, modulate, eps):
    o_ref = rest[-1]
    x = x_ref[0].astype(F32)
    ms = jnp.mean(x * x, axis=-1, keepdims=True)
    y = x * lax.rsqrt(ms + eps) * g_ref[...]
    if modulate:
        sh_ref, sc_ref = rest[0], rest[1]
        y = y * (1.0 + sc_ref[0, 0]) + sh_ref[0, 0]
    o_ref[0] = y.astype(o_ref.dtype)


def _rms_norm(x, g, out_dtype, mod=None, layer=0, chunks=None, row_of_batch=None):
    B, S, D = x.shape
    ts = _tile(S, 256, 8)
    in_specs = [pl.BlockSpec((1, ts, D), lambda b, s: (b, s, 0)),
                pl.BlockSpec((1, D), lambda b, s: (0, 0))]
    args = [x, g.reshape(1, D)]
    if mod is not None:
        for ch in chunks:
            in_specs.append(pl.BlockSpec((1, 1, 1, D), lambda b, s, ch=ch: (layer, row_of_batch(b), 0, ch)))
            args.append(mod)
    need = 2 * ts * D * (4 + jnp.dtype(out_dtype).itemsize) + 4 * ts * D * 4
    return pl.pallas_call(
        functools.partial(_norm_kernel, modulate=mod is not None, eps=NORM_EPS),
        grid=(B, S // ts),
        in_specs=in_specs,
        out_specs=pl.BlockSpec((1, ts, D), lambda b, s: (b, s, 0)),
        out_shape=jax.ShapeDtypeStruct((B, S, D), out_dtype),
        compiler_params=_cparams(("parallel", "parallel"), need),
        name="rms_norm",
    )(*args)


def _rope(x, cos, sin_lo, sin_hi):
    up = pltpu.roll(x, 96, 1)
    dn = pltpu.roll(x, 32, 1)
    return x * cos + up * sin_lo + dn * sin_hi


def _mm_heads_kernel(x_ref, w_ref, *rest, hw, n_rope):
    o_ref = rest[-1]
    acc = _dot(x_ref[...], w_ref[...])
    bb, nh, sr, _ = o_ref.shape

    def store(rope):
        for bi in range(bb):
            for c in range(nh):
                blk = acc[bi * sr:(bi + 1) * sr, c * hw:(c + 1) * hw]
                if rope:
                    cos, s_lo, s_hi = rest[0][...], rest[1][...], rest[2][...]
                    blk = jnp.concatenate(
                        [_rope(blk[:, t * 128:(t + 1) * 128], cos, s_lo, s_hi) for t in range(hw // 128)], axis=-1)
                o_ref[bi, c] = blk.astype(o_ref.dtype)

    if n_rope == 0:
        store(False)
    else:
        j = pl.program_id(1)

        @pl.when(j < n_rope)
        def _():
            store(True)

        @pl.when(j >= n_rope)
        def _():
            store(False)


def _mm_heads(x, w, layer, col_off, n_cols, batch, hw, out_dtype, rope_tabs=None, n_rope_cols=0):
    M, K = x.shape
    S = M // batch
    bm = _tile(M, 1024, 8)
    bn = _tile(K, 1024)
    if bm >= S:
        bb, sr = bm // S, S
    else:
        bb, sr = 1, bm
    sblocks = S // sr
    joff = col_off // bn
    nh = bn // hw
    in_specs = [pl.BlockSpec((bm, K), lambda i, j: (i, 0)),
                pl.BlockSpec((None, K, bn), lambda i, j: (layer, 0, joff + j))]
    args = [x, w]
    n_rope = 0
    if rope_tabs is not None and n_rope_cols:
        n_rope = n_rope_cols // bn
        for t in rope_tabs:
            in_specs.append(pl.BlockSpec((sr, 128), lambda i, j: (i % sblocks, 0)))
            args.append(t)
    osz = jnp.dtype(out_dtype).itemsize
    need = 2 * (bm * K * 2 + K * bn * 2 + bm * bn * osz) + 3 * bm * bn * 4
    return pl.pallas_call(
        functools.partial(_mm_heads_kernel, hw=hw, n_rope=n_rope),
        grid=(M // bm, n_cols // bn),
        in_specs=in_specs,
        out_specs=pl.BlockSpec((bb, nh, sr, hw), lambda i, j: (i // sblocks, j, i % sblocks, 0)),
        out_shape=jax.ShapeDtypeStruct((batch, n_cols // hw, S, hw), out_dtype),
        compiler_params=_cparams(("parallel", "arbitrary"), need),
        name="proj_heads",
    )(*args)


def _mm_resid_kernel(x_ref, w_ref, r_ref, g_ref, o_ref):
    o_ref[...] = r_ref[...] + g_ref[0, 0] * _dot(x_ref[...], w_ref[...])


def _mm_resid(x, w, layer, res, mod, chunk, row_of_tile, bm_want, bn_want):
    M, K = x.shape
    N = w.shape[-1]
    bm = _tile(M, bm_want, 8)
    bn = _tile(N, bn_want)
    gblk = chunk * (N // bn)
    need = 2 * (bm * K * 2 + K * bn * 2 + 2 * bm * bn * 4) + 2 * bm * bn * 4
    return pl.pallas_call(
        _mm_resid_kernel,
        grid=(M // bm, N // bn),
        in_specs=[pl.BlockSpec((bm, K), lambda i, j: (i, 0)),
                  pl.BlockSpec((None, K, bn), lambda i, j: (layer, 0, j)),
                  pl.BlockSpec((bm, bn), lambda i, j: (i, j)),
                  pl.BlockSpec((1, 1, 1, bn), lambda i, j: (layer, row_of_tile(i, bm), 0, gblk + j))],
        out_specs=pl.BlockSpec((bm, bn), lambda i, j: (i, j)),
        out_shape=jax.ShapeDtypeStruct((M, N), F32),
        compiler_params=_cparams(("parallel", "arbitrary"), need),
        name="proj_residual",
    )(x, w, res, mod)


def _upconv_kernel(x_ref, wa_ref, wb_ref, cwa_ref, cwb_ref, cba_ref, cbb_ref, o_ref, *, seq):
    x = x_ref[...]
    bm = x.shape[0]
    a = _dot(x, wa_ref[...])
    b = _dot(x, wb_ref[...])
    row = lax.broadcasted_iota(jnp.int32, a.shape, 0)
    pos = row & (seq - 1) if seq & (seq - 1) == 0 else lax.rem(row, jnp.int32(seq))
    first = pos == 0
    last = pos == seq - 1

    def conv(u, cw, cb):
        prev = jnp.where(first, 0.0, pltpu.roll(u, 1, 0))
        nxt = jnp.where(last, 0.0, pltpu.roll(u, bm - 1, 0))
        return prev * cw[0:1] + u * cw[1:2] + nxt * cw[2:3] + cb

    ua = conv(a, cwa_ref[...], cba_ref[...])
    ub = conv(b, cwb_ref[...], cbb_ref[...])
    o_ref[...] = (ua * jax.nn.sigmoid(ua) * ub).astype(o_ref.dtype)


def _upconv(h, w_up, conv_w, conv_b, layer, seq):
    M, K = h.shape
    F2 = w_up.shape[-1]
    F = F2 // 2
    bm = _tile(M, max(2048, seq), seq)
    tf = _tile(F, 256)
    nb = F // tf
    need = bm * K * 2 + 2 * (2 * K * tf * 2 + bm * tf * 2) + 10 * bm * tf * 4
    return pl.pallas_call(
        functools.partial(_upconv_kernel, seq=seq),
        grid=(M // bm, nb),
        in_specs=[pl.BlockSpec((bm, K), lambda i, j: (i, 0), pipeline_mode=pl.Buffered(1)),
                  pl.BlockSpec((None, K, tf), lambda i, j: (layer, 0, j)),
                  pl.BlockSpec((None, K, tf), lambda i, j: (layer, 0, nb + j)),
                  pl.BlockSpec((None, 3, tf), lambda i, j: (layer, 0, j)),
                  pl.BlockSpec((None, 3, tf), lambda i, j: (layer, 0, nb + j)),
                  pl.BlockSpec((None, 1, tf), lambda i, j: (layer, 0, j)),
                  pl.BlockSpec((None, 1, tf), lambda i, j: (layer, 0, nb + j))],
        out_specs=pl.BlockSpec((bm, tf), lambda i, j: (i, j)),
        out_shape=jax.ShapeDtypeStruct((M, F), BF16),
        compiler_params=_cparams(("parallel", "arbitrary"), need),
        name="ffn_up_conv_gate",
    )(h, w_up, w_up, conv_w, conv_w, conv_b, conv_b)


def _softmax_parts(parts):
    m = parts[0].max(axis=-1, keepdims=True)
    for p in parts[1:]:
        m = jnp.maximum(m, p.max(axis=-1, keepdims=True))
    es = [jnp.exp(p - m) for p in parts]
    l = es[0].sum(axis=-1, keepdims=True)
    for e in es[1:]:
        l = l + e.sum(axis=-1, keepdims=True)
    return es, l


def _diff_lambda(p, lam_init):
    t1 = jnp.sum(p[0:1] * p[1:2], axis=-1, keepdims=True)
    t2 = jnp.sum(p[2:3] * p[3:4], axis=-1, keepdims=True)
    return jnp.exp(t1) - jnp.exp(t2) + lam_init


def _head_norm(o, g, lam_init):
    y = o * lax.rsqrt(jnp.mean(o * o, axis=-1, keepdims=True) + DIFF_NORM_EPS)
    return y * g * (1.0 - lam_init)


def _ctx_dense_kernel(q_ref, k_ref, v_ref, o_ref, *, scale):
    hb, d = q_ref.shape[1], q_ref.shape[3]
    for hh in range(hb):
        q = q_ref[0, hh]
        k = k_ref[0, hh].astype(BF16)
        v = v_ref[0, hh].astype(BF16)
        (e,), l = _softmax_parts([_dot_nt(q, k) * scale])
        o = _dot(e.astype(BF16), v) / l
        o_ref[0, :, hh * d:(hh + 1) * d] = o.astype(o_ref.dtype)


def _ctx_dense_attention(q, k, v):
    B, H, S, d = q.shape
    hb = _tile(H, 8, 1)
    spec = pl.BlockSpec((1, hb, S, d), lambda b, h: (b, h, 0, 0))
    need = 2 * hb * S * d * (2 + 4 + 4 + 2) + 6 * S * S * 4
    return pl.pallas_call(
        functools.partial(_ctx_dense_kernel, scale=d ** -0.5),
        grid=(B, H // hb),
        in_specs=[spec, spec, spec],
        out_specs=pl.BlockSpec((1, S, hb * d), lambda b, h: (b, 0, h)),
        out_shape=jax.ShapeDtypeStruct((B, S, H * d), BF16),
        compiler_params=_cparams(("parallel", "parallel"), need),
        name="ctx_dense_attention",
    )(q, k, v)


def _ctx_diff_kernel(lam_ref, g_ref, q_ref, k_ref, v_ref, o_ref, *, scale, lam_init):
    hb, dv = q_ref.shape[1], v_ref.shape[3]
    dq = lam_ref.shape[-1]
    lam = _diff_lambda(lam_ref[0], lam_init)
    g = g_ref[...]
    for hh in range(hb):
        ps = []
        for c in range(2):
            q = q_ref[0, hh, :, c * dq:(c + 1) * dq]
            k = k_ref[0, hh, :, c * dq:(c + 1) * dq].astype(BF16)
            (e,), l = _softmax_parts([_dot_nt(q, k) * scale])
            ps.append(e / l)
        w = (ps[0] - lam * ps[1]).astype(BF16)
        o = _dot(w, v_ref[0, hh].astype(BF16))
        o_ref[0, :, hh * dv:(hh + 1) * dv] = _head_norm(o, g, lam_init).astype(o_ref.dtype)


def _ctx_diff_attention(q, k, v, lam_params, subln, idx, lam_init):
    B, H, S, dv = v.shape
    dq = lam_params.shape[-1]
    hb = _tile(H, 4, 1)
    spec = pl.BlockSpec((1, hb, S, dv), lambda b, h: (b, h, 0, 0))
    need = 2 * hb * S * dv * (2 + 4 + 4 + 2) + 10 * S * S * 4
    return pl.pallas_call(
        functools.partial(_ctx_diff_kernel, scale=dq ** -0.5, lam_init=lam_init),
        grid=(B, H // hb),
        in_specs=[pl.BlockSpec((1, 4, dq), lambda b, h: (idx, 0, 0)),
                  pl.BlockSpec((None, 1, dv), lambda b, h: (idx, 0, 0)),
                  spec, spec, spec],
        out_specs=pl.BlockSpec((1, S, hb * dv), lambda b, h: (b, 0, h)),
        out_shape=jax.ShapeDtypeStruct((B, S, H * dv), BF16),
        compiler_params=_cparams(("parallel", "parallel"), need),
        name="ctx_diff_attention",
    )(lam_params, subln, q, k, v)


def _na_tables(rows, win_r, win_c):
    G, W = NA_GROUP_ROWS, GRID_W
    wr = min(win_r, rows)
    kr = min(rows, wr + G)
    ngroups = rows // G
    qr_l, qc = np.divmod(np.arange(G * W), W)
    kr_l, kc = np.divmod(np.arange(kr * W), W)
    cs = np.clip(qc - win_c // 2, 0, W - win_c)
    col_ok = (kc[None, :] >= cs[:, None]) & (kc[None, :] < cs[:, None] + win_c)
    dc_idx = np.clip(kc[None, :] - qc[:, None], -(win_c - 1), win_c - 1) + win_c - 1

    def block(g):
        ks = int(np.clip(G * g - wr // 2, 0, rows - kr))
        qr = G * g + qr_l
        rs = np.clip(qr - wr // 2, 0, rows - wr)
        krow = ks + kr_l
        row_ok = (krow[None, :] >= rs[:, None]) & (krow[None, :] < rs[:, None] + wr)
        dr_idx = np.clip(krow[None, :] - qr[:, None] + win_r - 1, 0, 2 * win_r - 2)
        return ks, dr_idx, row_ok & col_ok

    reps = sorted({0, min(1, ngroups - 1), ngroups - 1})
    pats = [block(g) for g in reps]
    starts, pat_of = [], []
    for g in range(ngroups):
        ks, dr_idx, ok = block(g)
        pid = 0 if g == 0 else (len(reps) - 1 if g == ngroups - 1 else 1)
        assert np.array_equal(ok, pats[pid][2]) and np.array_equal(dr_idx[ok], pats[pid][1][ok])
        starts.append(ks)
        pat_of.append(pid)
    dr = np.stack([p[1] for p in pats])
    ok = np.stack([p[2] for p in pats])
    assert (ok.sum(-1) == wr * win_c).all()
    dc = np.broadcast_to(dc_idx, dr.shape)
    return kr, starts, pat_of, dr, dc, ok


def _na_kernel(q_ref, k_ref, v_ref, kc_ref, vc_ref, bias_ref, o_ref, *, scale, starts, pat_of, kr):
    gq = NA_GROUP_ROWS * GRID_W
    kc = kc_ref[0, 0].astype(BF16)
    vc = vc_ref[0, 0].astype(BF16)
    for g, (ks, pid) in enumerate(zip(starts, pat_of)):
        q = q_ref[0, 0, g * gq:(g + 1) * gq]
        kw = k_ref[0, 0, ks * GRID_W:(ks + kr) * GRID_W]
        vw = v_ref[0, 0, ks * GRID_W:(ks + kr) * GRID_W]
        s_loc = _dot_nt(q, kw) * scale + bias_ref[0, pid]
        s_ctx = _dot_nt(q, kc) * scale
        (e_loc, e_ctx), l = _softmax_parts([s_loc, s_ctx])
        o = (_dot(e_loc.astype(BF16), vw) + _dot(e_ctx.astype(BF16), vc)) / l
        o_ref[0, g * gq:(g + 1) * gq, :] = o.astype(o_ref.dtype)


def _na_attention(qkv, k_ctx, v_ctx, rpb):
    B, H3, N, d = qkv.shape
    H = H3 // 3
    P = k_ctx.shape[2]
    rows = N // GRID_W
    win_r, win_c = (rpb.shape[1] + 1) // 2, (rpb.shape[2] + 1) // 2
    kr, starts, pat_of, dr, dc, ok = _na_tables(rows, win_r, win_c)
    bias = jnp.where(ok[None], rpb[:, dr, dc].astype(F32), NEG_INF)
    npat, gq, gk = bias.shape[1:]
    need = 2 * (3 * N * d * 2 + 2 * P * d * 4 + npat * gq * gk * 4 + N * d * 2) + 8 * gq * (gk + P) * 4
    return pl.pallas_call(
        functools.partial(_na_kernel, scale=d ** -0.5, starts=tuple(starts), pat_of=tuple(pat_of), kr=kr),
        grid=(H, B),
        in_specs=[pl.BlockSpec((1, 1, N, d), lambda h, b: (b, h, 0, 0)),
                  pl.BlockSpec((1, 1, N, d), lambda h, b: (b, H + h, 0, 0)),
                  pl.BlockSpec((1, 1, N, d), lambda h, b: (b, 2 * H + h, 0, 0)),
                  pl.BlockSpec((1, 1, P, d), lambda h, b: (b, h, 0, 0)),
                  pl.BlockSpec((1, 1, P, d), lambda h, b: (b, h, 0, 0)),
                  pl.BlockSpec((1, npat, gq, gk), lambda h, b: (h, 0, 0, 0))],
        out_specs=pl.BlockSpec((1, N, d), lambda h, b: (b, 0, h)),
        out_shape=jax.ShapeDtypeStruct((B, N, H * d), BF16),
        compiler_params=_cparams(("parallel", "arbitrary"), need),
        name="neighbourhood_attention",
    )(qkv, qkv, qkv, k_ctx, v_ctx, bias)


def _lat_diff_kernel(lam_ref, g_ref, q_ref, k_ref, v_ref, kc_ref, vc_ref, o_ref, *, scale, lam_init):
    dq = lam_ref.shape[-1]
    lam = _diff_lambda(lam_ref[0], lam_init)
    p_lat, p_ctx = [], []
    for c in range(2):
        q = q_ref[0, 0, :, c * dq:(c + 1) * dq]
        k = k_ref[0, 0, :, c * dq:(c + 1) * dq]
        kc = kc_ref[0, 0, :, c * dq:(c + 1) * dq].astype(BF16)
        (e_lat, e_ctx), l = _softmax_parts([_dot_nt(q, k) * scale, _dot_nt(q, kc) * scale])
        p_lat.append(e_lat / l)
        p_ctx.append(e_ctx / l)
    w_lat = (p_lat[0] - lam * p_lat[1]).astype(BF16)
    w_ctx = (p_ctx[0] - lam * p_ctx[1]).astype(BF16)
    o = _dot(w_lat, v_ref[0, 0]) + _dot(w_ctx, vc_ref[0, 0].astype(BF16))
    o_ref[0] = _head_norm(o, g_ref[...], lam_init).astype(o_ref.dtype)


def _lat_diff_attention(qkv, k_ctx, v_ctx, lam_params, subln, idx, lam_init):
    B, H3, N, dv = qkv.shape
    H = H3 // 3
    P = k_ctx.shape[2]
    dq = lam_params.shape[-1]
    tq = _tile(N, 256, 8)
    need = 2 * (tq * dv * 4 + 2 * N * dv * 2 + 2 * P * dv * 4) + 12 * tq * (N + P) * 4
    return pl.pallas_call(
        functools.partial(_lat_diff_kernel, scale=dq ** -0.5, lam_init=lam_init),
        grid=(B, H, N // tq),
        in_specs=[pl.BlockSpec((1, 4, dq), lambda b, h, t: (idx, 0, 0)),
                  pl.BlockSpec((None, 1, dv), lambda b, h, t: (idx, 0, 0)),
                  pl.BlockSpec((1, 1, tq, dv), lambda b, h, t: (b, h, t, 0)),
                  pl.BlockSpec((1, 1, N, dv), lambda b, h, t: (b, H + h, 0, 0)),
                  pl.BlockSpec((1, 1, N, dv), lambda b, h, t: (b, 2 * H + h, 0, 0)),
                  pl.BlockSpec((1, 1, P, dv), lambda b, h, t: (b, h, 0, 0)),
                  pl.BlockSpec((1, 1, P, dv), lambda b, h, t: (b, h, 0, 0))],
        out_specs=pl.BlockSpec((1, tq, dv), lambda b, h, t: (b, t, h)),
        out_shape=jax.ShapeDtypeStruct((B, N, H * dv), BF16),
        compiler_params=_cparams(("parallel", "parallel", "arbitrary"), need),
        name="latent_diff_attention",
    )(lam_params, subln, qkv, qkv, qkv, k_ctx, v_ctx)


def _rope_tables(n_lat, dq):
    half = dq // 2
    nf = half // 2
    t = jnp.arange(n_lat)
    pos_r = (t // GRID_W).astype(F32)
    pos_c = (t % GRID_W).astype(F32)
    inv = ROPE_THETA ** (-jnp.arange(nf, dtype=F32) / nf)
    zeros = jnp.zeros((n_lat, nf), F32)

    def tabs(pos):
        ang = pos[:, None] * inv[None, :]
        cos, sin = jnp.cos(ang), jnp.sin(ang)
        return (jnp.concatenate([cos, cos], -1), jnp.concatenate([-sin, zeros], -1),
                jnp.concatenate([zeros, sin], -1))

    r, c = tabs(pos_r), tabs(pos_c)
    return tuple(jnp.concatenate([a, b], -1) for a, b in zip(r, c))


def kernel(x_prompt, x_sample, cache_k_0, cache_v_0, cache_k_1, cache_v_1, cache_k_2, cache_v_2, cache_k_3, cache_v_3, c, c_ctx, norm_attn, norm_ffn, w_ada, b_ada, w_qkv, w_o, na_rpb, diff_lambda, diff_subln, w_up, conv_w, conv_b, w_down, final_norm):
    caches = [(cache_k_0, cache_v_0), (cache_k_1, cache_v_1), (cache_k_2, cache_v_2), (cache_k_3, cache_v_3)]
    Bp, Sp, D = x_prompt.shape
    Bs, Ns, _ = x_sample.shape
    L = w_qkv.shape[0]
    Hn = na_rpb.shape[1]
    dn = D // Hn
    dq = diff_lambda.shape[-1]
    dv = 2 * dq
    Hd = D // dv
    assert Ns % (GRID_W * NA_GROUP_ROWS) == 0 and dq == 128 and dn % 128 == 0

    n_cond = 1 + Bs
    cond = jnp.concatenate([c_ctx[None, :], c, jnp.zeros((-n_cond % 16, D), F32)], axis=0)
    mod = _ada_params(cond, w_ada, b_ada).reshape(L, cond.shape[0], 1, 6 * D)

    w_qkv_b, w_o_b = w_qkv.astype(BF16), w_o.astype(BF16)
    w_up_b, w_down_b = w_up.astype(BF16), w_down.astype(BF16)
    conv_b3 = conv_b.reshape(L, 1, conv_b.shape[-1])
    diff_subln = diff_subln.reshape(diff_subln.shape[0], 1, dv)
    rope_tabs = _rope_tables(Ns, dq)

    ctx_row = lambda b: 0
    ctx_tile_row = lambda i, bm: 0
    lat_row = lambda b: 1 + b
    lat_tile_row = lambda i, bm: 1 + (i * bm) // Ns

    xp = x_prompt.reshape(Bp * Sp, D)
    xs = x_sample.reshape(Bs * Ns, D)
    new_state = []
    for l in range(L):
        is_na = l % 2 == 0
        hw = dn if is_na else dv
        H = Hn if is_na else Hd
        lam_init = 0.8 - 0.6 * math.exp(-0.3 * l)

        h = _rms_norm(xp.reshape(Bp, Sp, D), norm_attn[l], BF16, mod, l, (0, 1), ctx_row).reshape(Bp * Sp, D)
        q = _mm_heads(h, w_qkv_b, l, 0, D, Bp, hw, BF16)
        k_new = _mm_heads(h, w_qkv_b, l, D, D, Bp, hw, F32)
        v_new = _mm_heads(h, w_qkv_b, l, 2 * D, D, Bp, hw, F32)
        if is_na:
            o = _ctx_dense_attention(q, k_new, v_new)
            new_state += [k_new, v_new]
        else:
            o = _ctx_diff_attention(q, k_new, v_new, diff_lambda, diff_subln, l // 2, lam_init)
            new_state += [k_new.reshape(Bp, H, Sp, 2, dq), v_new]
        xp = _mm_resid(o.reshape(Bp * Sp, D), w_o_b, l, xp, mod, 2, ctx_tile_row, 1024, 1024)
        h = _rms_norm(xp.reshape(Bp, Sp, D), norm_ffn[l], BF16, mod, l, (3, 4), ctx_row).reshape(Bp * Sp, D)
        u = _upconv(h, w_up_b, conv_w, conv_b3, l, Sp)
        xp = _mm_resid(u, w_down_b, l, xp, mod, 5, ctx_tile_row, 512, 512)

        k_ctx, v_ctx = caches[l]
        h = _rms_norm(xs.reshape(Bs, Ns, D), norm_attn[l], BF16, mod, l, (0, 1), lat_row).reshape(Bs * Ns, D)
        if is_na:
            qkv = _mm_heads(h, w_qkv_b, l, 0, 3 * D, Bs, hw, BF16)
            o = _na_attention(qkv, k_ctx, v_ctx, na_rpb[l // 2])
        else:
            qkv = _mm_heads(h, w_qkv_b, l, 0, 3 * D, Bs, hw, BF16, rope_tabs, 2 * D)
            k_ctx = k_ctx.reshape(k_ctx.shape[0], H, k_ctx.shape[2], dv)
            o = _lat_diff_attention(qkv, k_ctx, v_ctx, diff_lambda, diff_subln, l // 2, lam_init)
        xs = _mm_resid(o.reshape(Bs * Ns, D), w_o_b, l, xs, mod, 2, lat_tile_row, 1024, 1024)
        h = _rms_norm(xs.reshape(Bs, Ns, D), norm_ffn[l], BF16, mod, l, (3, 4), lat_row).reshape(Bs * Ns, D)
        u = _upconv(h, w_up_b, conv_w, conv_b3, l, Ns)
        xs = _mm_resid(u, w_down_b, l, xs, mod, 5, lat_tile_row, 512, 512)

    y_prompt = _rms_norm(xp.reshape(Bp, Sp, D), final_norm, F32)
    y_sample = _rms_norm(xs.reshape(Bs, Ns, D), final_norm, F32)
    return (y_prompt, y_sample, *new_state)
```

```python
import functools
import math

import numpy as np
import jax
import jax.numpy as jnp
from jax import lax
from jax.experimental import pallas as pl
from jax.experimental.pallas import tpu as pltpu

GRID_W = 64
ROPE_THETA = 10000.0
NORM_EPS = 1e-6
DIFF_NORM_EPS = 1e-5
NEG_INF = -1e30
NA_GROUP_ROWS = 4
V7X_VMEM_BYTES = 64 * 1024 * 1024
V7X_VMEM_RESERVE = 6 * 1024 * 1024
V7X_LANES = 128
V7X_MXU_COLS = 256
ATTN_ROWS = 256
F32 = jnp.float32
BF16 = jnp.bfloat16


def _cparams(semantics, need_bytes):
    limit = int(min(V7X_VMEM_BYTES - V7X_VMEM_RESERVE, max(need_bytes, 16 * 1024 * 1024)))
    return pltpu.CompilerParams(dimension_semantics=semantics, vmem_limit_bytes=limit)


def _tile(n, want, quantum=V7X_LANES):
    if n <= want:
        return n
    t = (want // quantum) * quantum
    while t >= quantum:
        if n % t == 0:
            return t
        t -= quantum
    return n


def _dot(a, b):
    return jnp.dot(a, b, preferred_element_type=F32)


def _dot_nt(a, b):
    return lax.dot_general(a, b, (((1,), (1,)), ((), ())), preferred_element_type=F32)


def _ada_kernel(c_ref, w_ref, b_ref, o_ref):
    c = c_ref[...]
    a = (c * jax.nn.sigmoid(c)).astype(BF16)
    o_ref[0] = _dot(a, w_ref[0].astype(BF16)) + b_ref[0]


def _ada_params(cond, w_ada, b_ada):
    L, D, N = w_ada.shape
    R = cond.shape[0]
    tn = _tile(N, 512)
    need = 2 * (D * tn * 4 + R * tn * 8) + 2 * R * D * 4 + D * tn * 2
    return pl.pallas_call(
        _ada_kernel,
        grid=(L, N // tn),
        in_specs=[pl.BlockSpec((R, D), lambda l, j: (0, 0)),
                  pl.BlockSpec((1, D, tn), lambda l, j: (l, 0, j)),
                  pl.BlockSpec((1, 1, tn), lambda l, j: (l, 0, j))],
        out_specs=pl.BlockSpec((1, R, tn), lambda l, j: (l, 0, j)),
        out_shape=jax.ShapeDtypeStruct((L, R, N), F32),
        compiler_params=_cparams(("parallel", "parallel"), need),
        name="ada_params",
    )(cond, w_ada, b_ada.reshape(L, 1, N))


def _norm_kernel(x_ref, g_ref, *rest, modulate, eps):
    o_ref = rest[-1]
    x = x_ref[0].astype(F32)
    ms = jnp.mean(x * x, axis=-1, keepdims=True)
    y = x * lax.rsqrt(ms + eps) * g_ref[...]
    if modulate:
        sh_ref, sc_ref = rest[0], rest[1]
        y = y * (1.0 + sc_ref[0, 0]) + sh_ref[0, 0]
    o_ref[0] = y.astype(o_ref.dtype)


def _rms_norm(x, g, out_dtype, mod=None, layer=0, chunks=None, row_of_batch=None):
    B, S, D = x.shape
    ts = _tile(S, 256, 8)
    in_specs = [pl.BlockSpec((1, ts, D), lambda b, s: (b, s, 0)),
                pl.BlockSpec((1, D), lambda b, s: (0, 0))]
    args = [x, g.reshape(1, D)]
    if mod is not None:
        for ch in chunks:
            in_specs.append(pl.BlockSpec((1, 1, 1, D), lambda b, s, ch=ch: (layer, row_of_batch(b), 0, ch)))
            args.append(mod)
    need = 2 * ts * D * (4 + jnp.dtype(out_dtype).itemsize) + 4 * ts * D * 4
    return pl.pallas_call(
        functools.partial(_norm_kernel, modulate=mod is not None, eps=NORM_EPS),
        grid=(B, S // ts),
        in_specs=in_specs,
        out_specs=pl.BlockSpec((1, ts, D), lambda b, s: (b, s, 0)),
        out_shape=jax.ShapeDtypeStruct((B, S, D), out_dtype),
        compiler_params=_cparams(("parallel", "parallel"), need),
        name="rms_norm",
    )(*args)


def _rope(x, cos, sin_lo, sin_hi):
    up = pltpu.roll(x, 96, 1)
    dn = pltpu.roll(x, 32, 1)
    return x * cos + up * sin_lo + dn * sin_hi


def _mm_heads_kernel(x_ref, w_ref, *rest, hw, sect_blocks, first_sect, n_sect, rope, q_scale):
    o_ref = rest[-1]
    bb, nh, sr, _ = o_ref.shape
    cw = max(hw, V7X_MXU_COLS)
    heads_per_chunk = cw // hw

    def body(sect):
        for n in range(nh // heads_per_chunk):
            acc = _dot(x_ref[...], w_ref[:, n * cw:(n + 1) * cw])
            for hh in range(heads_per_chunk):
                for bi in range(bb):
                    blk = acc[bi * sr:(bi + 1) * sr, hh * hw:(hh + 1) * hw]
                    if rope and sect < 2:
                        cos, s_lo, s_hi = rest[0][...], rest[1][...], rest[2][...]
                        blk = jnp.concatenate(
                            [_rope(blk[:, t * V7X_LANES:(t + 1) * V7X_LANES], cos, s_lo, s_hi)
                             for t in range(hw // V7X_LANES)], axis=-1)
                    if sect == 0:
                        blk = blk * q_scale
                    o_ref[bi, n * heads_per_chunk + hh] = blk.astype(o_ref.dtype)

    if n_sect == 1:
        body(first_sect)
    else:
        sect_id = pl.program_id(1) // sect_blocks
        for s in range(first_sect, first_sect + n_sect):
            pl.when(sect_id == s - first_sect)(functools.partial(body, s))


def _mm_heads(x, w, layer, first_sect, n_sect, batch, hw, out_dtype, q_scale, rope_tabs=None):
    M, K = x.shape
    S = M // batch
    bm = _tile(M, 1024, 8)
    bn = _tile(K, 1024)
    sect_blocks = K // bn
    if bm >= S:
        bb, sr = bm // S, S
    else:
        bb, sr = 1, bm
    sblocks = S // sr
    joff = first_sect * sect_blocks
    nh = bn // hw
    in_specs = [pl.BlockSpec((bm, K), lambda i, j: (i, 0)),
                pl.BlockSpec((None, K, bn), lambda i, j: (layer, 0, joff + j))]
    args = [x, w]
    if rope_tabs is not None:
        for t in rope_tabs:
            in_specs.append(pl.BlockSpec((sr, V7X_LANES), lambda i, j: (i % sblocks, 0)))
            args.append(t)
    osz = jnp.dtype(out_dtype).itemsize
    need = 2 * (bm * K * 2 + K * bn * 2 + bm * bn * osz) + 6 * bm * V7X_MXU_COLS * 4
    return pl.pallas_call(
        functools.partial(_mm_heads_kernel, hw=hw, sect_blocks=sect_blocks, first_sect=first_sect, n_sect=n_sect,
                          rope=rope_tabs is not None, q_scale=q_scale),
        grid=(M // bm, n_sect * sect_blocks),
        in_specs=in_specs,
        out_specs=pl.BlockSpec((bb, nh, sr, hw), lambda i, j: (i // sblocks, j, i % sblocks, 0)),
        out_shape=jax.ShapeDtypeStruct((batch, n_sect * K // hw, S, hw), out_dtype),
        compiler_params=_cparams(("parallel", "arbitrary"), need),
        name="proj_heads",
    )(*args)


def _mm_resid_kernel(x_ref, w_ref, r_ref, g_ref, o_ref):
    bn = o_ref.shape[1]
    cw = min(bn, V7X_MXU_COLS)
    for n in range(bn // cw):
        cols = slice(n * cw, (n + 1) * cw)
        o_ref[:, cols] = r_ref[:, cols] + g_ref[0, 0, :, cols] * _dot(x_ref[...], w_ref[:, cols])


def _mm_resid(x, w, layer, res, mod, chunk, row_of_tile, bm_want, bn_want):
    M, K = x.shape
    N = w.shape[-1]
    bm = _tile(M, bm_want, 8)
    bn = _tile(N, bn_want)
    gblk = chunk * (N // bn)
    need = 2 * (bm * K * 2 + K * bn * 2 + 2 * bm * bn * 4) + 4 * bm * V7X_MXU_COLS * 4
    return pl.pallas_call(
        _mm_resid_kernel,
        grid=(M // bm, N // bn),
        in_specs=[pl.BlockSpec((bm, K), lambda i, j: (i, 0)),
                  pl.BlockSpec((None, K, bn), lambda i, j: (layer, 0, j)),
                  pl.BlockSpec((bm, bn), lambda i, j: (i, j)),
                  pl.BlockSpec((1, 1, 1, bn), lambda i, j: (layer, row_of_tile(i, bm), 0, gblk + j))],
        out_specs=pl.BlockSpec((bm, bn), lambda i, j: (i, j)),
        out_shape=jax.ShapeDtypeStruct((M, N), F32),
        compiler_params=_cparams(("parallel", "arbitrary"), need),
        name="proj_residual",
    )(x, w, res, mod)


def _upconv_kernel(x_ref, wa_ref, wb_ref, cwa_ref, cwb_ref, cba_ref, cbb_ref, o_ref, *, seq):
    x = x_ref[...]
    bm = x.shape[0]
    a = _dot(x, wa_ref[...])
    b = _dot(x, wb_ref[...])
    row = lax.broadcasted_iota(jnp.int32, a.shape, 0)
    pos = row & (seq - 1) if seq & (seq - 1) == 0 else lax.rem(row, jnp.int32(seq))
    first = pos == 0
    last = pos == seq - 1

    def conv(u, cw, cb):
        prev = jnp.where(first, 0.0, pltpu.roll(u, 1, 0))
        nxt = jnp.where(last, 0.0, pltpu.roll(u, bm - 1, 0))
        return prev * cw[0:1] + u * cw[1:2] + nxt * cw[2:3] + cb

    ua = conv(a, cwa_ref[...], cba_ref[...])
    ub = conv(b, cwb_ref[...], cbb_ref[...])
    o_ref[...] = (ua * jax.nn.sigmoid(ua) * ub).astype(o_ref.dtype)


def _upconv(h, w_up, conv_w, conv_b, layer, seq):
    M, K = h.shape
    F2 = w_up.shape[-1]
    F = F2 // 2
    bm = _tile(M, max(2048, seq), seq)
    tf = _tile(F, V7X_MXU_COLS)
    nb = F // tf
    need = bm * K * 2 + 2 * (2 * K * tf * 2 + bm * tf * 2) + 10 * bm * tf * 4
    return pl.pallas_call(
        functools.partial(_upconv_kernel, seq=seq),
        grid=(M // bm, nb),
        in_specs=[pl.BlockSpec((bm, K), lambda i, j: (i, 0), pipeline_mode=pl.Buffered(1)),
                  pl.BlockSpec((None, K, tf), lambda i, j: (layer, 0, j)),
                  pl.BlockSpec((None, K, tf), lambda i, j: (layer, 0, nb + j)),
                  pl.BlockSpec((None, 3, tf), lambda i, j: (layer, 0, j)),
                  pl.BlockSpec((None, 3, tf), lambda i, j: (layer, 0, nb + j)),
                  pl.BlockSpec((None, 1, tf), lambda i, j: (layer, 0, j)),
                  pl.BlockSpec((None, 1, tf), lambda i, j: (layer, 0, nb + j))],
        out_specs=pl.BlockSpec((bm, tf), lambda i, j: (i, j)),
        out_shape=jax.ShapeDtypeStruct((M, F), BF16),
        compiler_params=_cparams(("parallel", "arbitrary"), need),
        name="ffn_up_conv_gate",
    )(h, w_up, w_up, conv_w, conv_w, conv_b, conv_b)


def _softmax_parts(parts):
    m = parts[0].max(axis=-1, keepdims=True)
    for p in parts[1:]:
        m = jnp.maximum(m, p.max(axis=-1, keepdims=True))
    es = [jnp.exp(p - m) for p in parts]
    l = es[0].sum(axis=-1, keepdims=True)
    for e in es[1:]:
        l = l + e.sum(axis=-1, keepdims=True)
    return es, l


def _diff_lambda(p, lam_init):
    t1 = jnp.sum(p[0:1] * p[1:2], axis=-1, keepdims=True)
    t2 = jnp.sum(p[2:3] * p[3:4], axis=-1, keepdims=True)
    return jnp.exp(t1) - jnp.exp(t2) + lam_init


def _head_norm(o, g, lam_init):
    y = o * lax.rsqrt(jnp.mean(o * o, axis=-1, keepdims=True) + DIFF_NORM_EPS)
    return y * g * (1.0 - lam_init)


def _ctx_dense_kernel(q_ref, k_ref, v_ref, o_ref):
    hb, d = q_ref.shape[1], q_ref.shape[3]
    sm = [_softmax_parts([s]) for s in [_dot_nt(q_ref[0, hh], k_ref[0, hh].astype(BF16)) for hh in range(hb)]]
    for hh, ((e,), l) in enumerate(sm):
        o = _dot(e.astype(BF16), v_ref[0, hh].astype(BF16)) / l
        o_ref[0, :, hh * d:(hh + 1) * d] = o.astype(o_ref.dtype)


def _ctx_dense_attention(q, k, v):
    B, H, S, d = q.shape
    hb = _tile(H, 8, 1)
    spec = pl.BlockSpec((1, hb, S, d), lambda b, h: (b, h, 0, 0))
    need = 2 * hb * S * d * (2 + 4 + 4 + 2) + 6 * S * S * 4
    return pl.pallas_call(
        _ctx_dense_kernel,
        grid=(B, H // hb),
        in_specs=[spec, spec, spec],
        out_specs=pl.BlockSpec((1, S, hb * d), lambda b, h: (b, 0, h)),
        out_shape=jax.ShapeDtypeStruct((B, S, H * d), BF16),
        compiler_params=_cparams(("parallel", "parallel"), need),
        name="ctx_dense_attention",
    )(q, k, v)


def _ctx_diff_kernel(lam_ref, g_ref, q_ref, k_ref, v_ref, o_ref, *, lam_init):
    hb, dv = q_ref.shape[1], v_ref.shape[3]
    dq = lam_ref.shape[-1]
    lam = _diff_lambda(lam_ref[0], lam_init)
    g = g_ref[...]
    scores = [_dot_nt(q_ref[0, hh, :, c * dq:(c + 1) * dq], k_ref[0, hh, :, c * dq:(c + 1) * dq].astype(BF16))
              for hh in range(hb) for c in range(2)]
    sm = [_softmax_parts([s]) for s in scores]
    for hh in range(hb):
        v = v_ref[0, hh].astype(BF16)
        os = [_dot(e.astype(BF16), v) / l for (e,), l in sm[2 * hh:2 * hh + 2]]
        o_ref[0, :, hh * dv:(hh + 1) * dv] = _head_norm(os[0] - lam * os[1], g, lam_init).astype(o_ref.dtype)


def _ctx_diff_attention(q, k, v, lam_params, subln, idx, lam_init):
    B, H, S, dv = v.shape
    dq = lam_params.shape[-1]
    hb = _tile(H, 4, 1)
    spec = pl.BlockSpec((1, hb, S, dv), lambda b, h: (b, h, 0, 0))
    need = 2 * hb * S * dv * (2 + 4 + 4 + 2) + 10 * S * S * 4
    return pl.pallas_call(
        functools.partial(_ctx_diff_kernel, lam_init=lam_init),
        grid=(B, H // hb),
        in_specs=[pl.BlockSpec((1, 4, dq), lambda b, h: (idx, 0, 0)),
                  pl.BlockSpec((None, 1, dv), lambda b, h: (idx, 0, 0)),
                  spec, spec, spec],
        out_specs=pl.BlockSpec((1, S, hb * dv), lambda b, h: (b, 0, h)),
        out_shape=jax.ShapeDtypeStruct((B, S, H * dv), BF16),
        compiler_params=_cparams(("parallel", "parallel"), need),
        name="ctx_diff_attention",
    )(lam_params, subln, q, k, v)


def _na_tables(rows, win_r, win_c):
    G, W = NA_GROUP_ROWS, GRID_W
    wr = min(win_r, rows)
    kr = min(rows, wr + G)
    ngroups = rows // G
    assert rows % G == 0 and kr % 2 == 0 and 2 * W == V7X_LANES and 2 * win_c - 1 <= W

    def block(g):
        ks = int(np.clip(G * g - wr // 2, 0, rows - kr))
        qr = G * g + np.arange(G)
        rs = np.clip(qr - wr // 2, 0, rows - wr)
        krow = ks + np.arange(kr)
        ok = (krow[None, :] >= rs[:, None]) & (krow[None, :] < rs[:, None] + wr)
        dr = np.where(ok, krow[None, :] - qr[:, None] + win_r - 1, -1)
        assert (ok.sum(1) == wr).all() and dr.max() <= 2 * win_r - 2 and (dr[ok] >= 0).all()
        return ks, dr

    reps = sorted({0, min(1, ngroups - 1), ngroups - 1})
    pats = [block(g)[1] for g in reps]
    starts, pat_of = [], []
    for g in range(ngroups):
        ks, dr = block(g)
        pid = 0 if g == 0 else (len(reps) - 1 if g == ngroups - 1 else 1)
        assert np.array_equal(dr, pats[pid])
        starts.append(ks)
        pat_of.append(pid)
    qc, kc = np.arange(W)[:, None], np.arange(W)[None, :]
    cs = np.clip(qc - win_c // 2, 0, W - win_c)
    col_ok = (kc >= cs) & (kc < cs + win_c)
    assert (np.abs(kc - qc)[col_ok] <= win_c - 1).all()
    colneg = np.where(col_ok, 0.0, NEG_INF).astype(np.float32)
    return kr, tuple(starts), tuple(pat_of), pats, np.concatenate([colneg, colneg], axis=1)


def _na_build_bias(rp_ref, colneg_ref, bias_ref, pats, win_c):
    W = GRID_W
    colneg = colneg_ref[...]
    lane = lax.broadcasted_iota(jnp.int32, (W, V7X_LANES), 1)
    cache = {}

    def tile(d_lo, d_hi):
        if (d_lo, d_hi) not in cache:
            if d_lo < 0 and d_hi < 0:
                val = jnp.full((W, V7X_LANES), NEG_INF, F32)
            else:
                src = None
                if d_lo >= 0:
                    src = jnp.broadcast_to(rp_ref[0, d_lo:d_lo + 1, :], (W, V7X_LANES))
                if d_hi >= 0:
                    hi = pltpu.roll(jnp.broadcast_to(rp_ref[0, d_hi:d_hi + 1, :], (W, V7X_LANES)), W, 1)
                    src = hi if src is None else src + hi
                val = pltpu.roll(src, V7X_LANES - (win_c - 1), 1, stride=1, stride_axis=0) + colneg
                if d_lo < 0:
                    val = jnp.where(lane < W, NEG_INF, val)
                if d_hi < 0:
                    val = jnp.where(lane >= W, NEG_INF, val)
            cache[(d_lo, d_hi)] = val
        return cache[(d_lo, d_hi)]

    for p, dr in enumerate(pats):
        for qr in range(dr.shape[0]):
            for m in range(dr.shape[1] // 2):
                bias_ref[p, qr * W:(qr + 1) * W, m * V7X_LANES:(m + 1) * V7X_LANES] = tile(
                    int(dr[qr, 2 * m]), int(dr[qr, 2 * m + 1]))


def _na_kernel(q_ref, k_ref, v_ref, kc_ref, vc_ref, rp_ref, colneg_ref, o_ref, bias_ref, *, starts, pat_of, pats, kr,
               win_c):
    @pl.when(pl.program_id(1) == 0)
    def _():
        _na_build_bias(rp_ref, colneg_ref, bias_ref, pats, win_c)

    gq = NA_GROUP_ROWS * GRID_W
    kc = kc_ref[0, 0].astype(BF16)
    vc = vc_ref[0, 0].astype(BF16)
    scores = []
    for g, (ks, pid) in enumerate(zip(starts, pat_of)):
        q = q_ref[0, 0, g * gq:(g + 1) * gq]
        kw = k_ref[0, 0, ks * GRID_W:(ks + kr) * GRID_W]
        scores.append([_dot_nt(q, kw) + bias_ref[pid], _dot_nt(q, kc)])
    sm = [_softmax_parts(s) for s in scores]
    for g, (ks, (es, l)) in enumerate(zip(starts, sm)):
        vw = v_ref[0, 0, ks * GRID_W:(ks + kr) * GRID_W]
        o = (_dot(es[0].astype(BF16), vw) + _dot(es[1].astype(BF16), vc)) / l
        o_ref[0, g * gq:(g + 1) * gq, :] = o.astype(o_ref.dtype)


def _na_attention(qkv, k_ctx, v_ctx, rpb):
    B, H3, N, d = qkv.shape
    H = H3 // 3
    P = k_ctx.shape[2]
    rows = N // GRID_W
    win_r, win_c = (rpb.shape[1] + 1) // 2, (rpb.shape[2] + 1) // 2
    kr, starts, pat_of, pats, colneg = _na_tables(rows, win_r, win_c)
    rp = jnp.pad(rpb.astype(F32), ((0, 0), (0, -rpb.shape[1] % 8), (0, V7X_LANES - rpb.shape[2])))
    gq, gk = NA_GROUP_ROWS * GRID_W, kr * GRID_W
    need = (2 * (3 * N * d * 2 + 2 * P * d * 4 + N * d * 2) + len(pats) * gq * gk * 4
            + 3 * len(starts) * gq * (gk + P) * 4)
    return pl.pallas_call(
        functools.partial(_na_kernel, starts=starts, pat_of=pat_of, pats=pats, kr=kr, win_c=win_c),
        grid=(H, B),
        in_specs=[pl.BlockSpec((1, 1, N, d), lambda h, b: (b, h, 0, 0)),
                  pl.BlockSpec((1, 1, N, d), lambda h, b: (b, H + h, 0, 0)),
                  pl.BlockSpec((1, 1, N, d), lambda h, b: (b, 2 * H + h, 0, 0)),
                  pl.BlockSpec((1, 1, P, d), lambda h, b: (b, h, 0, 0)),
                  pl.BlockSpec((1, 1, P, d), lambda h, b: (b, h, 0, 0)),
                  pl.BlockSpec((1,) + rp.shape[1:], lambda h, b: (h, 0, 0)),
                  pl.BlockSpec(colneg.shape, lambda h, b: (0, 0))],
        out_specs=pl.BlockSpec((1, N, d), lambda h, b: (b, 0, h)),
        out_shape=jax.ShapeDtypeStruct((B, N, H * d), BF16),
        scratch_shapes=[pltpu.VMEM((len(pats), gq, gk), F32)],
        compiler_params=_cparams(("parallel", "arbitrary"), need),
        name="neighbourhood_attention",
    )(qkv, qkv, qkv, k_ctx, v_ctx, rp, jnp.asarray(colneg))


def _lat_diff_kernel(lam_ref, g_ref, q_ref, k_ref, v_ref, kc_ref, vc_ref, o_ref, *, lam_init):
    dq = lam_ref.shape[-1]
    tq = q_ref.shape[2]
    sub = min(tq, ATTN_ROWS)
    lam = _diff_lambda(lam_ref[0], lam_init)
    v = v_ref[0, 0]
    vc = vc_ref[0, 0].astype(BF16)
    scores = []
    for t in range(tq // sub):
        for c in range(2):
            cols = slice(c * dq, (c + 1) * dq)
            q = q_ref[0, 0, t * sub:(t + 1) * sub, cols]
            scores.append([_dot_nt(q, k_ref[0, 0, :, cols]), _dot_nt(q, kc_ref[0, 0, :, cols].astype(BF16))])
    sm = [_softmax_parts(s) for s in scores]
    os = [(_dot(es[0].astype(BF16), v) + _dot(es[1].astype(BF16), vc)) / l for es, l in sm]
    for t in range(tq // sub):
        o = os[2 * t] - lam * os[2 * t + 1]
        o_ref[0, t * sub:(t + 1) * sub, :] = _head_norm(o, g_ref[...], lam_init).astype(o_ref.dtype)


def _lat_diff_attention(qkv, k_ctx, v_ctx, lam_params, subln, idx, lam_init):
    B, H3, N, dv = qkv.shape
    H = H3 // 3
    P = k_ctx.shape[2]
    dq = lam_params.shape[-1]
    tq = _tile(N, 2 * ATTN_ROWS, 8)
    need = 2 * (tq * dv * 4 + 2 * N * dv * 2 + 2 * P * dv * 4) + 8 * tq * (N + P) * 4
    return pl.pallas_call(
        functools.partial(_lat_diff_kernel, lam_init=lam_init),
        grid=(B, H, N // tq),
        in_specs=[pl.BlockSpec((1, 4, dq), lambda b, h, t: (idx, 0, 0)),
                  pl.BlockSpec((None, 1, dv), lambda b, h, t: (idx, 0, 0)),
                  pl.BlockSpec((1, 1, tq, dv), lambda b, h, t: (b, h, t, 0)),
                  pl.BlockSpec((1, 1, N, dv), lambda b, h, t: (b, H + h, 0, 0)),
                  pl.BlockSpec((1, 1, N, dv), lambda b, h, t: (b, 2 * H + h, 0, 0)),
                  pl.BlockSpec((1, 1, P, dv), lambda b, h, t: (b, h, 0, 0)),
                  pl.BlockSpec((1, 1, P, dv), lambda b, h, t: (b, h, 0, 0))],
        out_specs=pl.BlockSpec((1, tq, dv), lambda b, h, t: (b, t, h)),
        out_shape=jax.ShapeDtypeStruct((B, N, H * dv), BF16),
        compiler_params=_cparams(("parallel", "parallel", "arbitrary"), need),
        name="latent_diff_attention",
    )(lam_params, subln, qkv, qkv, qkv, k_ctx, v_ctx)


def _rope_tables(n_lat, dq):
    half = dq // 2
    nf = half // 2
    t = jnp.arange(n_lat)
    pos_r = (t // GRID_W).astype(F32)
    pos_c = (t % GRID_W).astype(F32)
    inv = ROPE_THETA ** (-jnp.arange(nf, dtype=F32) / nf)
    zeros = jnp.zeros((n_lat, nf), F32)

    def tabs(pos):
        ang = pos[:, None] * inv[None, :]
        cos, sin = jnp.cos(ang), jnp.sin(ang)
        return (jnp.concatenate([cos, cos], -1), jnp.concatenate([-sin, zeros], -1),
                jnp.concatenate([zeros, sin], -1))

    r, c = tabs(pos_r), tabs(pos_c)
    return tuple(jnp.concatenate([a, b], -1) for a, b in zip(r, c))


def kernel(x_prompt, x_sample, cache_k_0, cache_v_0, cache_k_1, cache_v_1, cache_k_2, cache_v_2, cache_k_3, cache_v_3, c, c_ctx, norm_attn, norm_ffn, w_ada, b_ada, w_qkv, w_o, na_rpb, diff_lambda, diff_subln, w_up, conv_w, conv_b, w_down, final_norm):
    caches = [(cache_k_0, cache_v_0), (cache_k_1, cache_v_1), (cache_k_2, cache_v_2), (cache_k_3, cache_v_3)]
    Bp, Sp, D = x_prompt.shape
    Bs, Ns, _ = x_sample.shape
    L = w_qkv.shape[0]
    Hn = na_rpb.shape[1]
    dn = D // Hn
    dq = diff_lambda.shape[-1]
    dv = 2 * dq
    Hd = D // dv
    assert dq == V7X_LANES and dn % V7X_LANES == 0

    n_cond = 1 + Bs
    cond = jnp.concatenate([c_ctx[None, :], c, jnp.zeros((-n_cond % 16, D), F32)], axis=0)
    mod = _ada_params(cond, w_ada, b_ada).reshape(L, cond.shape[0], 1, 6 * D)

    w_qkv_b, w_o_b = w_qkv.astype(BF16), w_o.astype(BF16)
    w_up_b, w_down_b = w_up.astype(BF16), w_down.astype(BF16)
    conv_b3 = conv_b.reshape(L, 1, conv_b.shape[-1])
    diff_subln = diff_subln.reshape(diff_subln.shape[0], 1, dv)
    rope_tabs = _rope_tables(Ns, dq)

    ctx_row = lambda b: 0
    ctx_tile_row = lambda i, bm: 0
    lat_row = lambda b: 1 + b
    lat_tile_row = lambda i, bm: 1 + (i * bm) // Ns

    xp = x_prompt.reshape(Bp * Sp, D)
    xs = x_sample.reshape(Bs * Ns, D)
    new_state = []
    for l in range(L):
        is_na = l % 2 == 0
        hw = dn if is_na else dv
        H = Hn if is_na else Hd
        q_scale = (dn if is_na else dq) ** -0.5
        lam_init = 0.8 - 0.6 * math.exp(-0.3 * l)

        h = _rms_norm(xp.reshape(Bp, Sp, D), norm_attn[l], BF16, mod, l, (0, 1), ctx_row).reshape(Bp * Sp, D)
        q = _mm_heads(h, w_qkv_b, l, 0, 1, Bp, hw, BF16, q_scale)
        k_new = _mm_heads(h, w_qkv_b, l, 1, 1, Bp, hw, F32, q_scale)
        v_new = _mm_heads(h, w_qkv_b, l, 2, 1, Bp, hw, F32, q_scale)
        if is_na:
            o = _ctx_dense_attention(q, k_new, v_new)
            new_state += [k_new, v_new]
        else:
            o = _ctx_diff_attention(q, k_new, v_new, diff_lambda, diff_subln, l // 2, lam_init)
            new_state += [k_new.reshape(Bp, H, Sp, 2, dq), v_new]
        xp = _mm_resid(o.reshape(Bp * Sp, D), w_o_b, l, xp, mod, 2, ctx_tile_row, 1024, 1024)
        h = _rms_norm(xp.reshape(Bp, Sp, D), norm_ffn[l], BF16, mod, l, (3, 4), ctx_row).reshape(Bp * Sp, D)
        u = _upconv(h, w_up_b, conv_w, conv_b3, l, Sp)
        xp = _mm_resid(u, w_down_b, l, xp, mod, 5, ctx_tile_row, 512, 512)

        k_ctx, v_ctx = caches[l]
        h = _rms_norm(xs.reshape(Bs, Ns, D), norm_attn[l], BF16, mod, l, (0, 1), lat_row).reshape(Bs * Ns, D)
        if is_na:
            qkv = _mm_heads(h, w_qkv_b, l, 0, 3, Bs, hw, BF16, q_scale)
            o = _na_attention(qkv, k_ctx, v_ctx, na_rpb[l // 2])
        else:
            qkv = _mm_heads(h, w_qkv_b, l, 0, 3, Bs, hw, BF16, q_scale, rope_tabs)
            k_ctx = k_ctx.reshape(k_ctx.shape[0], H, k_ctx.shape[2], dv)
            o = _lat_diff_attention(qkv, k_ctx, v_ctx, diff_lambda, diff_subln, l // 2, lam_init)
        xs = _mm_resid(o.reshape(Bs * Ns, D), w_o_b, l, xs, mod, 2, lat_tile_row, 1024, 1024)
        h = _rms_norm(xs.reshape(Bs, Ns, D), norm_ffn[l], BF16, mod, l, (3, 4), lat_row).reshape(Bs * Ns, D)
        u = _upconv(h, w_up_b, conv_w, conv_b3, l, Ns)
        xs = _mm_resid(u, w_down_b, l, xs, mod, 5, lat_tile_row, 512, 512)

    y_prompt = _rms_norm(xp.reshape(Bp, Sp, D), final_norm, F32)
    y_sample = _rms_norm(xs.reshape(Bs, Ns, D), final_norm, F32)
    return (y_prompt, y_sample, *new_state)
```

```python
import functools
import math

import numpy as np
import jax
import jax.numpy as jnp
from jax import lax
from jax.experimental import pallas as pl
from jax.experimental.pallas import tpu as pltpu

GRID_W = 64
ROPE_THETA = 10000.0
NORM_EPS = 1e-6
DIFF_NORM_EPS = 1e-5
NEG_INF = -1e30
NA_GROUP_ROWS = 4
V7X_VMEM_BYTES = 64 * 1024 * 1024
V7X_VMEM_RESERVE = 6 * 1024 * 1024
V7X_LANES = 128
V7X_MXU_COLS = 256
ATTN_ROWS = 256
F32 = jnp.float32
BF16 = jnp.bfloat16


def _cparams(semantics, need_bytes):
    limit = int(min(V7X_VMEM_BYTES - V7X_VMEM_RESERVE, max(need_bytes, 16 * 1024 * 1024)))
    return pltpu.CompilerParams(dimension_semantics=semantics, vmem_limit_bytes=limit)


def _tile(n, want, quantum=V7X_LANES):
    if n <= want:
        return n
    t = (want // quantum) * quantum
    while t >= quantum:
        if n % t == 0:
            return t
        t -= quantum
    return n


def _dot(a, b):
    return jnp.dot(a, b, preferred_element_type=F32)


def _dot_nt(a, b):
    return lax.dot_general(a, b, (((1,), (1,)), ((), ())), preferred_element_type=F32)


def _ada_kernel(c_ref, w_ref, b_ref, o_ref):
    c = c_ref[...]
    a = (c * jax.nn.sigmoid(c)).astype(BF16)
    o_ref[0] = _dot(a, w_ref[0].astype(BF16)) + b_ref[0]


def _ada_params(cond, w_ada, b_ada):
    L, D, N = w_ada.shape
    R = cond.shape[0]
    tn = _tile(N, 512)
    need = 2 * (D * tn * 4 + R * tn * 8) + 2 * R * D * 4 + D * tn * 2
    return pl.pallas_call(
        _ada_kernel,
        grid=(L, N // tn),
        in_specs=[pl.BlockSpec((R, D), lambda l, j: (0, 0)),
                  pl.BlockSpec((1, D, tn), lambda l, j: (l, 0, j)),
                  pl.BlockSpec((1, 1, tn), lambda l, j: (l, 0, j))],
        out_specs=pl.BlockSpec((1, R, tn), lambda l, j: (l, 0, j)),
        out_shape=jax.ShapeDtypeStruct((L, R, N), F32),
        compiler_params=_cparams(("parallel", "parallel"), need),
        name="ada_params",
    )(cond, w_ada, b_ada.reshape(L, 1, N))


def _norm_kernel(x_ref, g_ref, *rest, modulate, eps):
    o_ref = rest[-1]
    x = x_ref[0].astype(F32)
    ms = jnp.mean(x * x, axis=-1, keepdims=True)
    y = x * lax.rsqrt(ms + eps) * g_ref[...]
    if modulate:
        sh_ref, sc_ref = rest[0], rest[1]
        y = y * (1.0 + sc_ref[0, 0]) + sh_ref[0, 0]
    o_ref[0] = y.astype(o_ref.dtype)


def _rms_norm(x, g, out_dtype, mod=None, layer=0, chunks=None, row_of_batch=None):
    B, S, D = x.shape
    ts = _tile(S, 512, 8)
    in_specs = [pl.BlockSpec((1, ts, D), lambda b, s: (b, s, 0)),
                pl.BlockSpec((1, D), lambda b, s: (0, 0))]
    args = [x, g.reshape(1, D)]
    if mod is not None:
        for ch in chunks:
            in_specs.append(pl.BlockSpec((1, 1, 1, D), lambda b, s, ch=ch: (layer, row_of_batch(b), 0, ch)))
            args.append(mod)
    need = 2 * ts * D * (4 + jnp.dtype(out_dtype).itemsize) + 4 * ts * D * 4
    return pl.pallas_call(
        functools.partial(_norm_kernel, modulate=mod is not None, eps=NORM_EPS),
        grid=(B, S // ts),
        in_specs=in_specs,
        out_specs=pl.BlockSpec((1, ts, D), lambda b, s: (b, s, 0)),
        out_shape=jax.ShapeDtypeStruct((B, S, D), out_dtype),
        compiler_params=_cparams(("parallel", "parallel"), need),
        name="rms_norm",
    )(*args)


def _rope(x, cos, sin_lo, sin_hi):
    up = pltpu.roll(x, 96, 1)
    dn = pltpu.roll(x, 32, 1)
    return x * cos + up * sin_lo + dn * sin_hi


def _mm_heads_kernel(x_ref, w_ref, *rest, hw, sect_blocks, first_sect, n_sect, rope, q_scale):
    o_ref = rest[-1]
    bb, nh, sr, _ = o_ref.shape
    cw = max(hw, V7X_MXU_COLS)
    heads_per_chunk = cw // hw

    def body(sect):
        for n in range(nh // heads_per_chunk):
            acc = _dot(x_ref[...], w_ref[:, n * cw:(n + 1) * cw])
            for hh in range(heads_per_chunk):
                for bi in range(bb):
                    blk = acc[bi * sr:(bi + 1) * sr, hh * hw:(hh + 1) * hw]
                    if rope and sect < 2:
                        cos, s_lo, s_hi = rest[0][...], rest[1][...], rest[2][...]
                        blk = jnp.concatenate(
                            [_rope(blk[:, t * V7X_LANES:(t + 1) * V7X_LANES], cos, s_lo, s_hi)
                             for t in range(hw // V7X_LANES)], axis=-1)
                    if sect == 0:
                        blk = blk * q_scale
                    o_ref[bi, n * heads_per_chunk + hh] = blk.astype(o_ref.dtype)

    if n_sect == 1:
        body(first_sect)
    else:
        sect_id = pl.program_id(1) // sect_blocks
        for s in range(first_sect, first_sect + n_sect):
            pl.when(sect_id == s - first_sect)(functools.partial(body, s))


def _mm_heads(x, w, layer, first_sect, n_sect, batch, hw, out_dtype, q_scale, rope_tabs=None):
    M, K = x.shape
    S = M // batch
    bm = _tile(M, 1024, 8)
    bn = _tile(K, 1024)
    sect_blocks = K // bn
    if bm >= S:
        bb, sr = bm // S, S
    else:
        bb, sr = 1, bm
    sblocks = S // sr
    joff = first_sect * sect_blocks
    nh = bn // hw
    in_specs = [pl.BlockSpec((bm, K), lambda i, j: (i, 0)),
                pl.BlockSpec((None, K, bn), lambda i, j: (layer, 0, joff + j))]
    args = [x, w]
    if rope_tabs is not None:
        for t in rope_tabs:
            in_specs.append(pl.BlockSpec((sr, V7X_LANES), lambda i, j: (i % sblocks, 0)))
            args.append(t)
    osz = jnp.dtype(out_dtype).itemsize
    need = 2 * (bm * K * 2 + K * bn * 2 + bm * bn * osz) + 6 * bm * V7X_MXU_COLS * 4
    return pl.pallas_call(
        functools.partial(_mm_heads_kernel, hw=hw, sect_blocks=sect_blocks, first_sect=first_sect, n_sect=n_sect,
                          rope=rope_tabs is not None, q_scale=q_scale),
        grid=(M // bm, n_sect * sect_blocks),
        in_specs=in_specs,
        out_specs=pl.BlockSpec((bb, nh, sr, hw), lambda i, j: (i // sblocks, j, i % sblocks, 0)),
        out_shape=jax.ShapeDtypeStruct((batch, n_sect * K // hw, S, hw), out_dtype),
        compiler_params=_cparams(("parallel", "arbitrary"), need),
        name="proj_heads",
    )(*args)


def _mm_resid_kernel(x_ref, w_ref, r_ref, g_ref, o_ref):
    bn = o_ref.shape[1]
    cw = min(bn, V7X_MXU_COLS)
    for n in range(bn // cw):
        cols = slice(n * cw, (n + 1) * cw)
        o_ref[:, cols] = r_ref[:, cols] + g_ref[0, 0, :, cols] * _dot(x_ref[...], w_ref[:, cols])


def _mm_resid(x, w, layer, res, mod, chunk, row_of_tile, bm_want, bn_want):
    M, K = x.shape
    N = w.shape[-1]
    bm = _tile(M, bm_want, 8)
    bn = _tile(N, bn_want)
    gblk = chunk * (N // bn)
    need = 2 * (bm * K * 2 + K * bn * 2 + 2 * bm * bn * 4) + 4 * bm * V7X_MXU_COLS * 4
    return pl.pallas_call(
        _mm_resid_kernel,
        grid=(M // bm, N // bn),
        in_specs=[pl.BlockSpec((bm, K), lambda i, j: (i, 0)),
                  pl.BlockSpec((None, K, bn), lambda i, j: (layer, 0, j)),
                  pl.BlockSpec((bm, bn), lambda i, j: (i, j)),
                  pl.BlockSpec((1, 1, 1, bn), lambda i, j: (layer, row_of_tile(i, bm), 0, gblk + j))],
        out_specs=pl.BlockSpec((bm, bn), lambda i, j: (i, j)),
        out_shape=jax.ShapeDtypeStruct((M, N), F32),
        compiler_params=_cparams(("parallel", "arbitrary"), need),
        name="proj_residual",
    )(x, w, res, mod)


def _upconv_kernel(x_ref, wa_ref, wb_ref, cwa_ref, cwb_ref, cba_ref, cbb_ref, o_ref, *w_out_refs, seq):
    x = x_ref[...]
    bm = x.shape[0]
    wa, wb = wa_ref[...], wb_ref[...]
    if w_out_refs:
        wa, wb = wa.astype(BF16), wb.astype(BF16)
        w_out_refs[0][...] = wa
        w_out_refs[1][...] = wb
    a = _dot(x, wa)
    b = _dot(x, wb)
    row = lax.broadcasted_iota(jnp.int32, a.shape, 0)
    pos = row & (seq - 1) if seq & (seq - 1) == 0 else lax.rem(row, jnp.int32(seq))
    first = pos == 0
    last = pos == seq - 1

    def conv(u, cw, cb):
        prev = jnp.where(first, 0.0, pltpu.roll(u, 1, 0))
        nxt = jnp.where(last, 0.0, pltpu.roll(u, bm - 1, 0))
        return prev * cw[0:1] + u * cw[1:2] + nxt * cw[2:3] + cb

    ua = conv(a, cwa_ref[...], cba_ref[...])
    ub = conv(b, cwb_ref[...], cbb_ref[...])
    o_ref[...] = (ua * jax.nn.sigmoid(ua) * ub).astype(o_ref.dtype)


def _upconv(h, w_up, conv_w, conv_b, layer, seq, w_halves=None):
    M, K = h.shape
    F = conv_w.shape[-1] // 2
    bm = _tile(M, max(2048, seq), seq)
    tf = _tile(F, V7X_MXU_COLS)
    nb = F // tf
    emit = w_halves is None
    wsz = 4 if emit else 2
    need = bm * K * 2 + 2 * (2 * K * tf * wsz + bm * tf * 2) + 10 * bm * tf * 4 + (6 * K * tf * 2 if emit else 0)
    if emit:
        w_args = (w_up, w_up)
        w_specs = [pl.BlockSpec((None, K, tf), lambda i, j: (layer, 0, j)),
                   pl.BlockSpec((None, K, tf), lambda i, j: (layer, 0, nb + j))]
    else:
        w_args = w_halves
        w_specs = [pl.BlockSpec((K, tf), lambda i, j: (0, j))] * 2
    out_specs = [pl.BlockSpec((bm, tf), lambda i, j: (i, j))]
    out_shape = [jax.ShapeDtypeStruct((M, F), BF16)]
    if emit:
        out_specs += [pl.BlockSpec((K, tf), lambda i, j: (0, j))] * 2
        out_shape += [jax.ShapeDtypeStruct((K, F), BF16)] * 2
    outs = pl.pallas_call(
        functools.partial(_upconv_kernel, seq=seq),
        grid=(M // bm, nb),
        in_specs=[pl.BlockSpec((bm, K), lambda i, j: (i, 0), pipeline_mode=pl.Buffered(1)), *w_specs,
                  pl.BlockSpec((None, 3, tf), lambda i, j: (layer, 0, j)),
                  pl.BlockSpec((None, 3, tf), lambda i, j: (layer, 0, nb + j)),
                  pl.BlockSpec((None, 1, tf), lambda i, j: (layer, 0, j)),
                  pl.BlockSpec((None, 1, tf), lambda i, j: (layer, 0, nb + j))],
        out_specs=out_specs,
        out_shape=out_shape,
        compiler_params=_cparams(("arbitrary", "arbitrary"), need),
        name="ffn_up_conv_gate",
    )(h, *w_args, conv_w, conv_w, conv_b, conv_b)
    return (outs[0], (outs[1], outs[2])) if emit else outs[0]


def _softmax_parts(parts):
    m = parts[0].max(axis=-1, keepdims=True)
    for p in parts[1:]:
        m = jnp.maximum(m, p.max(axis=-1, keepdims=True))
    es = [jnp.exp(p - m) for p in parts]
    l = es[0].sum(axis=-1, keepdims=True)
    for e in es[1:]:
        l = l + e.sum(axis=-1, keepdims=True)
    return es, l


def _diff_lambda(p, lam_init):
    t1 = jnp.sum(p[0:1] * p[1:2], axis=-1, keepdims=True)
    t2 = jnp.sum(p[2:3] * p[3:4], axis=-1, keepdims=True)
    return jnp.exp(t1) - jnp.exp(t2) + lam_init


def _head_norm(o, g, lam_init):
    y = o * lax.rsqrt(jnp.mean(o * o, axis=-1, keepdims=True) + DIFF_NORM_EPS)
    return y * g * (1.0 - lam_init)


def _ctx_dense_kernel(q_ref, k_ref, v_ref, o_ref):
    hb, d = q_ref.shape[1], q_ref.shape[3]
    sm = [_softmax_parts([s]) for s in [_dot_nt(q_ref[0, hh], k_ref[0, hh].astype(BF16)) for hh in range(hb)]]
    for hh, ((e,), l) in enumerate(sm):
        o = _dot(e.astype(BF16), v_ref[0, hh].astype(BF16)) / l
        o_ref[0, :, hh * d:(hh + 1) * d] = o.astype(o_ref.dtype)


def _ctx_dense_attention(q, k, v):
    B, H, S, d = q.shape
    hb = _tile(H, 8, 1)
    spec = pl.BlockSpec((1, hb, S, d), lambda b, h: (b, h, 0, 0))
    need = 2 * hb * S * d * (2 + 4 + 4 + 2) + 6 * S * S * 4
    return pl.pallas_call(
        _ctx_dense_kernel,
        grid=(B, H // hb),
        in_specs=[spec, spec, spec],
        out_specs=pl.BlockSpec((1, S, hb * d), lambda b, h: (b, 0, h)),
        out_shape=jax.ShapeDtypeStruct((B, S, H * d), BF16),
        compiler_params=_cparams(("parallel", "parallel"), need),
        name="ctx_dense_attention",
    )(q, k, v)


def _ctx_diff_kernel(lam_ref, g_ref, q_ref, k_ref, v_ref, o_ref, *, lam_init):
    hb, dv = q_ref.shape[1], v_ref.shape[3]
    dq = lam_ref.shape[-1]
    lam = _diff_lambda(lam_ref[0], lam_init)
    g = g_ref[...]
    scores = [_dot_nt(q_ref[0, hh, :, c * dq:(c + 1) * dq], k_ref[0, hh, :, c * dq:(c + 1) * dq].astype(BF16))
              for hh in range(hb) for c in range(2)]
    sm = [_softmax_parts([s]) for s in scores]
    for hh in range(hb):
        v = v_ref[0, hh].astype(BF16)
        os = [_dot(e.astype(BF16), v) / l for (e,), l in sm[2 * hh:2 * hh + 2]]
        o_ref[0, :, hh * dv:(hh + 1) * dv] = _head_norm(os[0] - lam * os[1], g, lam_init).astype(o_ref.dtype)


def _ctx_diff_attention(q, k, v, lam_params, subln, idx, lam_init):
    B, H, S, dv = v.shape
    dq = lam_params.shape[-1]
    hb = _tile(H, 4, 1)
    spec = pl.BlockSpec((1, hb, S, dv), lambda b, h: (b, h, 0, 0))
    need = 2 * hb * S * dv * (2 + 4 + 4 + 2) + 10 * S * S * 4
    return pl.pallas_call(
        functools.partial(_ctx_diff_kernel, lam_init=lam_init),
        grid=(B, H // hb),
        in_specs=[pl.BlockSpec((1, 4, dq), lambda b, h: (idx, 0, 0)),
                  pl.BlockSpec((None, 1, dv), lambda b, h: (idx, 0, 0)),
                  spec, spec, spec],
        out_specs=pl.BlockSpec((1, S, hb * dv), lambda b, h: (b, 0, h)),
        out_shape=jax.ShapeDtypeStruct((B, S, H * dv), BF16),
        compiler_params=_cparams(("parallel", "parallel"), need),
        name="ctx_diff_attention",
    )(lam_params, subln, q, k, v)


def _na_tables(rows, win_r, win_c):
    G, W = NA_GROUP_ROWS, GRID_W
    wr = min(win_r, rows)
    kr = min(rows, wr + G)
    ngroups = rows // G
    assert rows % G == 0 and kr % 2 == 0 and 2 * W == V7X_LANES and 2 * win_c - 1 <= W

    def block(g):
        ks = int(np.clip(G * g - wr // 2, 0, rows - kr))
        qr = G * g + np.arange(G)
        rs = np.clip(qr - wr // 2, 0, rows - wr)
        krow = ks + np.arange(kr)
        ok = (krow[None, :] >= rs[:, None]) & (krow[None, :] < rs[:, None] + wr)
        dr = np.where(ok, krow[None, :] - qr[:, None] + win_r - 1, -1)
        assert (ok.sum(1) == wr).all() and dr.max() <= 2 * win_r - 2 and (dr[ok] >= 0).all()
        return ks, dr

    reps = sorted({0, min(1, ngroups - 1), ngroups - 1})
    pats = [block(g)[1] for g in reps]
    starts, pat_of = [], []
    for g in range(ngroups):
        ks, dr = block(g)
        pid = 0 if g == 0 else (len(reps) - 1 if g == ngroups - 1 else 1)
        assert np.array_equal(dr, pats[pid])
        starts.append(ks)
        pat_of.append(pid)
    qc, kc = np.arange(W)[:, None], np.arange(W)[None, :]
    cs = np.clip(qc - win_c // 2, 0, W - win_c)
    col_ok = (kc >= cs) & (kc < cs + win_c)
    assert (np.abs(kc - qc)[col_ok] <= win_c - 1).all()
    colneg = np.where(col_ok, 0.0, NEG_INF).astype(np.float32)
    return kr, tuple(starts), tuple(pat_of), pats, np.concatenate([colneg, colneg], axis=1)


def _na_build_bias(rp_ref, colneg_ref, bias_ref, pats, win_c):
    W = GRID_W
    colneg = colneg_ref[...]
    lane = lax.broadcasted_iota(jnp.int32, (W, V7X_LANES), 1)
    cache = {}

    def tile(d_lo, d_hi):
        if (d_lo, d_hi) not in cache:
            if d_lo < 0 and d_hi < 0:
                val = jnp.full((W, V7X_LANES), NEG_INF, F32)
            else:
                src = None
                if d_lo >= 0:
                    src = jnp.broadcast_to(rp_ref[0, d_lo:d_lo + 1, :], (W, V7X_LANES))
                if d_hi >= 0:
                    hi = pltpu.roll(jnp.broadcast_to(rp_ref[0, d_hi:d_hi + 1, :], (W, V7X_LANES)), W, 1)
                    src = hi if src is None else src + hi
                val = pltpu.roll(src, V7X_LANES - (win_c - 1), 1, stride=1, stride_axis=0) + colneg
                if d_lo < 0:
                    val = jnp.where(lane < W, NEG_INF, val)
                if d_hi < 0:
                    val = jnp.where(lane >= W, NEG_INF, val)
            cache[(d_lo, d_hi)] = val
        return cache[(d_lo, d_hi)]

    for p, dr in enumerate(pats):
        for qr in range(dr.shape[0]):
            for m in range(dr.shape[1] // 2):
                bias_ref[p, qr * W:(qr + 1) * W, m * V7X_LANES:(m + 1) * V7X_LANES] = tile(
                    int(dr[qr, 2 * m]), int(dr[qr, 2 * m + 1]))


def _na_kernel(q_ref, k_ref, v_ref, kc_ref, vc_ref, rp_ref, colneg_ref, o_ref, bias_ref, *, starts, pat_of, pats, kr,
               win_c):
    @pl.when(pl.program_id(1) == 0)
    def _():
        _na_build_bias(rp_ref, colneg_ref, bias_ref, pats, win_c)

    gq = NA_GROUP_ROWS * GRID_W
    kc = kc_ref[0, 0].astype(BF16)
    vc = vc_ref[0, 0].astype(BF16)
    scores = []
    for g, (ks, pid) in enumerate(zip(starts, pat_of)):
        q = q_ref[0, 0, g * gq:(g + 1) * gq]
        kw = k_ref[0, 0, ks * GRID_W:(ks + kr) * GRID_W]
        scores.append([_dot_nt(q, kw) + bias_ref[pid], _dot_nt(q, kc)])
    sm = [_softmax_parts(s) for s in scores]
    for g, (ks, (es, l)) in enumerate(zip(starts, sm)):
        vw = v_ref[0, 0, ks * GRID_W:(ks + kr) * GRID_W]
        o = (_dot(es[0].astype(BF16), vw) + _dot(es[1].astype(BF16), vc)) / l
        o_ref[0, g * gq:(g + 1) * gq, :] = o.astype(o_ref.dtype)


def _na_attention(qkv, k_ctx, v_ctx, rpb):
    B, H3, N, d = qkv.shape
    H = H3 // 3
    P = k_ctx.shape[2]
    rows = N // GRID_W
    win_r, win_c = (rpb.shape[1] + 1) // 2, (rpb.shape[2] + 1) // 2
    kr, starts, pat_of, pats, colneg = _na_tables(rows, win_r, win_c)
    rp = jnp.pad(rpb.astype(F32), ((0, 0), (0, -rpb.shape[1] % 8), (0, V7X_LANES - rpb.shape[2])))
    gq, gk = NA_GROUP_ROWS * GRID_W, kr * GRID_W
    need = (2 * (3 * N * d * 2 + 2 * P * d * 4 + N * d * 2) + len(pats) * gq * gk * 4
            + 3 * len(starts) * gq * (gk + P) * 4)
    return pl.pallas_call(
        functools.partial(_na_kernel, starts=starts, pat_of=pat_of, pats=pats, kr=kr, win_c=win_c),
        grid=(H, B),
        in_specs=[pl.BlockSpec((1, 1, N, d), lambda h, b: (b, h, 0, 0)),
                  pl.BlockSpec((1, 1, N, d), lambda h, b: (b, H + h, 0, 0)),
                  pl.BlockSpec((1, 1, N, d), lambda h, b: (b, 2 * H + h, 0, 0)),
                  pl.BlockSpec((1, 1, P, d), lambda h, b: (b, h, 0, 0)),
                  pl.BlockSpec((1, 1, P, d), lambda h, b: (b, h, 0, 0)),
                  pl.BlockSpec((1,) + rp.shape[1:], lambda h, b: (h, 0, 0)),
                  pl.BlockSpec(colneg.shape, lambda h, b: (0, 0))],
        out_specs=pl.BlockSpec((1, N, d), lambda h, b: (b, 0, h)),
        out_shape=jax.ShapeDtypeStruct((B, N, H * d), BF16),
        scratch_shapes=[pltpu.VMEM((len(pats), gq, gk), F32)],
        compiler_params=_cparams(("parallel", "arbitrary"), need),
        name="neighbourhood_attention",
    )(qkv, qkv, qkv, k_ctx, v_ctx, rp, jnp.asarray(colneg))


def _lat_diff_kernel(lam_ref, g_ref, q_ref, k_ref, v_ref, kc_ref, vc_ref, o_ref, *, lam_init):
    dq = lam_ref.shape[-1]
    tq = q_ref.shape[2]
    sub = min(tq, ATTN_ROWS)
    lam = _diff_lambda(lam_ref[0], lam_init)
    v = v_ref[0, 0]
    vc = vc_ref[0, 0].astype(BF16)
    scores = []
    for t in range(tq // sub):
        for c in range(2):
            cols = slice(c * dq, (c + 1) * dq)
            q = q_ref[0, 0, t * sub:(t + 1) * sub, cols]
            scores.append([_dot_nt(q, k_ref[0, 0, :, cols]), _dot_nt(q, kc_ref[0, 0, :, cols].astype(BF16))])
    sm = [_softmax_parts(s) for s in scores]
    os = [(_dot(es[0].astype(BF16), v) + _dot(es[1].astype(BF16), vc)) / l for es, l in sm]
    for t in range(tq // sub):
        o = os[2 * t] - lam * os[2 * t + 1]
        o_ref[0, t * sub:(t + 1) * sub, :] = _head_norm(o, g_ref[...], lam_init).astype(o_ref.dtype)


def _lat_diff_attention(qkv, k_ctx, v_ctx, lam_params, subln, idx, lam_init):
    B, H3, N, dv = qkv.shape
    H = H3 // 3
    P = k_ctx.shape[2]
    dq = lam_params.shape[-1]
    tq = _tile(N, 4 * ATTN_ROWS, 8)
    need = 2 * (tq * dv * 4 + 2 * N * dv * 2 + 2 * P * dv * 4) + 8 * tq * (N + P) * 4
    return pl.pallas_call(
        functools.partial(_lat_diff_kernel, lam_init=lam_init),
        grid=(B, H, N // tq),
        in_specs=[pl.BlockSpec((1, 4, dq), lambda b, h, t: (idx, 0, 0)),
                  pl.BlockSpec((None, 1, dv), lambda b, h, t: (idx, 0, 0)),
                  pl.BlockSpec((1, 1, tq, dv), lambda b, h, t: (b, h, t, 0)),
                  pl.BlockSpec((1, 1, N, dv), lambda b, h, t: (b, H + h, 0, 0)),
                  pl.BlockSpec((1, 1, N, dv), lambda b, h, t: (b, 2 * H + h, 0, 0)),
                  pl.BlockSpec((1, 1, P, dv), lambda b, h, t: (b, h, 0, 0)),
                  pl.BlockSpec((1, 1, P, dv), lambda b, h, t: (b, h, 0, 0))],
        out_specs=pl.BlockSpec((1, tq, dv), lambda b, h, t: (b, t, h)),
        out_shape=jax.ShapeDtypeStruct((B, N, H * dv), BF16),
        compiler_params=_cparams(("parallel", "parallel", "arbitrary"), need),
        name="latent_diff_attention",
    )(lam_params, subln, qkv, qkv, qkv, k_ctx, v_ctx)


def _rope_tables(n_lat, dq):
    half = dq // 2
    nf = half // 2
    t = jnp.arange(n_lat)
    pos_r = (t // GRID_W).astype(F32)
    pos_c = (t % GRID_W).astype(F32)
    inv = ROPE_THETA ** (-jnp.arange(nf, dtype=F32) / nf)
    zeros = jnp.zeros((n_lat, nf), F32)

    def tabs(pos):
        ang = pos[:, None] * inv[None, :]
        cos, sin = jnp.cos(ang), jnp.sin(ang)
        return (jnp.concatenate([cos, cos], -1), jnp.concatenate([-sin, zeros], -1),
                jnp.concatenate([zeros, sin], -1))

    r, c = tabs(pos_r), tabs(pos_c)
    return tuple(jnp.concatenate([a, b], -1) for a, b in zip(r, c))


def kernel(x_prompt, x_sample, cache_k_0, cache_v_0, cache_k_1, cache_v_1, cache_k_2, cache_v_2, cache_k_3, cache_v_3, c, c_ctx, norm_attn, norm_ffn, w_ada, b_ada, w_qkv, w_o, na_rpb, diff_lambda, diff_subln, w_up, conv_w, conv_b, w_down, final_norm):
    caches = [(cache_k_0, cache_v_0), (cache_k_1, cache_v_1), (cache_k_2, cache_v_2), (cache_k_3, cache_v_3)]
    Bp, Sp, D = x_prompt.shape
    Bs, Ns, _ = x_sample.shape
    L = w_qkv.shape[0]
    Hn = na_rpb.shape[1]
    dn = D // Hn
    dq = diff_lambda.shape[-1]
    dv = 2 * dq
    Hd = D // dv
    assert dq == V7X_LANES and dn % V7X_LANES == 0

    n_cond = 1 + Bs
    cond = jnp.concatenate([c_ctx[None, :], c, jnp.zeros((-n_cond % 16, D), F32)], axis=0)
    mod = _ada_params(cond, w_ada, b_ada).reshape(L, cond.shape[0], 1, 6 * D)

    w_qkv_b, w_o_b = w_qkv.astype(BF16), w_o.astype(BF16)
    w_down_b = w_down.astype(BF16)
    conv_b3 = conv_b.reshape(L, 1, conv_b.shape[-1])
    diff_subln = diff_subln.reshape(diff_subln.shape[0], 1, dv)
    rope_tabs = _rope_tables(Ns, dq)

    ctx_row = lambda b: 0
    ctx_tile_row = lambda i, bm: 0
    lat_row = lambda b: 1 + b
    lat_tile_row = lambda i, bm: 1 + (i * bm) // Ns

    xp = x_prompt.reshape(Bp * Sp, D)
    xs = x_sample.reshape(Bs * Ns, D)
    new_state = []
    for l in range(L):
        is_na = l % 2 == 0
        hw = dn if is_na else dv
        H = Hn if is_na else Hd
        q_scale = (dn if is_na else dq) ** -0.5
        lam_init = 0.8 - 0.6 * math.exp(-0.3 * l)

        h = _rms_norm(xp.reshape(Bp, Sp, D), norm_attn[l], BF16, mod, l, (0, 1), ctx_row).reshape(Bp * Sp, D)
        q = _mm_heads(h, w_qkv_b, l, 0, 1, Bp, hw, BF16, q_scale)
        k_new = _mm_heads(h, w_qkv_b, l, 1, 1, Bp, hw, F32, q_scale)
        v_new = _mm_heads(h, w_qkv_b, l, 2, 1, Bp, hw, F32, q_scale)
        if is_na:
            o = _ctx_dense_attention(q, k_new, v_new)
            new_state += [k_new, v_new]
        else:
            o = _ctx_diff_attention(q, k_new, v_new, diff_lambda, diff_subln, l // 2, lam_init)
            new_state += [k_new.reshape(Bp, H, Sp, 2, dq), v_new]
        xp = _mm_resid(o.reshape(Bp * Sp, D), w_o_b, l, xp, mod, 2, ctx_tile_row, 1024, 1024)
        h = _rms_norm(xp.reshape(Bp, Sp, D), norm_ffn[l], BF16, mod, l, (3, 4), ctx_row).reshape(Bp * Sp, D)
        u, w_up_halves = _upconv(h, w_up, conv_w, conv_b3, l, Sp)
        xp = _mm_resid(u, w_down_b, l, xp, mod, 5, ctx_tile_row, 512, 512)

        k_ctx, v_ctx = caches[l]
        h = _rms_norm(xs.reshape(Bs, Ns, D), norm_attn[l], BF16, mod, l, (0, 1), lat_row).reshape(Bs * Ns, D)
        if is_na:
            qkv = _mm_heads(h, w_qkv_b, l, 0, 3, Bs, hw, BF16, q_scale)
            o = _na_attention(qkv, k_ctx, v_ctx, na_rpb[l // 2])
        else:
            qkv = _mm_heads(h, w_qkv_b, l, 0, 3, Bs, hw, BF16, q_scale, rope_tabs)
            k_ctx = k_ctx.reshape(k_ctx.shape[0], H, k_ctx.shape[2], dv)
            o = _lat_diff_attention(qkv, k_ctx, v_ctx, diff_lambda, diff_subln, l // 2, lam_init)
        xs = _mm_resid(o.reshape(Bs * Ns, D), w_o_b, l, xs, mod, 2, lat_tile_row, 1024, 1024)
        h = _rms_norm(xs.reshape(Bs, Ns, D), norm_ffn[l], BF16, mod, l, (3, 4), lat_row).reshape(Bs * Ns, D)
        u = _upconv(h, None, conv_w, conv_b3, l, Ns, w_up_halves)
        xs = _mm_resid(u, w_down_b, l, xs, mod, 5, lat_tile_row, 512, 512)

    y_prompt = _rms_norm(xp.reshape(Bp, Sp, D), final_norm, F32)
    y_sample = _rms_norm(xs.reshape(Bs, Ns, D), final_norm, F32)
    return (y_prompt, y_sample, *new_state)
```

```python
import functools
import math

import numpy as np
import jax
import jax.numpy as jnp
from jax import lax
from jax.experimental import pallas as pl
from jax.experimental.pallas import tpu as pltpu

GRID_W = 64
ROPE_THETA = 10000.0
NORM_EPS = 1e-6
DIFF_NORM_EPS = 1e-5
NEG_INF = -1e30
NA_GROUP_ROWS = 4
V7X_VMEM_BYTES = 64 * 1024 * 1024
V7X_VMEM_RESERVE = 6 * 1024 * 1024
V7X_LANES = 128
V7X_MXU_COLS = 256
ATTN_ROWS = 256
F32 = jnp.float32
BF16 = jnp.bfloat16


def _cparams(semantics, need_bytes):
    limit = int(min(V7X_VMEM_BYTES - V7X_VMEM_RESERVE, max(need_bytes, 16 * 1024 * 1024)))
    return pltpu.CompilerParams(dimension_semantics=semantics, vmem_limit_bytes=limit)


def _tile(n, want, quantum=V7X_LANES):
    if n <= want:
        return n
    t = (want // quantum) * quantum
    while t >= quantum:
        if n % t == 0:
            return t
        t -= quantum
    return n


def _dot(a, b):
    return jnp.dot(a, b, preferred_element_type=F32)


def _dot_nt(a, b):
    return lax.dot_general(a, b, (((1,), (1,)), ((), ())), preferred_element_type=F32)


def _ada_kernel(c_ref, w_ref, b_ref, o_ref):
    c = c_ref[...]
    a = (c * jax.nn.sigmoid(c)).astype(BF16)
    o_ref[0] = _dot(a, w_ref[0].astype(BF16)) + b_ref[0]


def _ada_params(cond, w_ada, b_ada):
    L, D, N = w_ada.shape
    R = cond.shape[0]
    tn = _tile(N, 512)
    need = 2 * (D * tn * 4 + R * tn * 8) + 2 * R * D * 4 + D * tn * 2
    return pl.pallas_call(
        _ada_kernel,
        grid=(L, N // tn),
        in_specs=[pl.BlockSpec((R, D), lambda l, j: (0, 0)),
                  pl.BlockSpec((1, D, tn), lambda l, j: (l, 0, j)),
                  pl.BlockSpec((1, 1, tn), lambda l, j: (l, 0, j))],
        out_specs=pl.BlockSpec((1, R, tn), lambda l, j: (l, 0, j)),
        out_shape=jax.ShapeDtypeStruct((L, R, N), F32),
        compiler_params=_cparams(("parallel", "parallel"), need),
        name="ada_params",
    )(cond, w_ada, b_ada.reshape(L, 1, N))


def _norm_kernel(x_ref, g_ref, *rest, modulate, eps):
    o_ref = rest[-1]
    x = x_ref[0].astype(F32)
    ms = jnp.mean(x * x, axis=-1, keepdims=True)
    y = x * lax.rsqrt(ms + eps) * g_ref[...]
    if modulate:
        sh_ref, sc_ref = rest[0], rest[1]
        y = y * (1.0 + sc_ref[0, 0]) + sh_ref[0, 0]
    o_ref[0] = y.astype(o_ref.dtype)


def _rms_norm(x, g, out_dtype, mod=None, layer=0, chunks=None, row_of_batch=None):
    B, S, D = x.shape
    ts = _tile(S, 512, 8)
    in_specs = [pl.BlockSpec((1, ts, D), lambda b, s: (b, s, 0)),
                pl.BlockSpec((1, D), lambda b, s: (0, 0))]
    args = [x, g.reshape(1, D)]
    if mod is not None:
        for ch in chunks:
            in_specs.append(pl.BlockSpec((1, 1, 1, D), lambda b, s, ch=ch: (layer, row_of_batch(b), 0, ch)))
            args.append(mod)
    need = 2 * ts * D * (4 + jnp.dtype(out_dtype).itemsize) + 4 * ts * D * 4
    return pl.pallas_call(
        functools.partial(_norm_kernel, modulate=mod is not None, eps=NORM_EPS),
        grid=(B, S // ts),
        in_specs=in_specs,
        out_specs=pl.BlockSpec((1, ts, D), lambda b, s: (b, s, 0)),
        out_shape=jax.ShapeDtypeStruct((B, S, D), out_dtype),
        compiler_params=_cparams(("parallel", "parallel"), need),
        name="rms_norm",
    )(*args)


def _rope(x, cos, sin_lo, sin_hi):
    up = pltpu.roll(x, 96, 1)
    dn = pltpu.roll(x, 32, 1)
    return x * cos + up * sin_lo + dn * sin_hi


def _mm_heads_kernel(x_ref, w_ref, *rest, hw, sect_blocks, first_sect, n_sect, rope, q_scale):
    o_ref = rest[-1]
    bb, nh, sr, _ = o_ref.shape
    cw = max(hw, V7X_MXU_COLS)
    heads_per_chunk = cw // hw

    def body(sect):
        for n in range(nh // heads_per_chunk):
            acc = _dot(x_ref[...], w_ref[:, n * cw:(n + 1) * cw])
            for hh in range(heads_per_chunk):
                for bi in range(bb):
                    blk = acc[bi * sr:(bi + 1) * sr, hh * hw:(hh + 1) * hw]
                    if rope and sect < 2:
                        cos, s_lo, s_hi = rest[0][...], rest[1][...], rest[2][...]
                        blk = jnp.concatenate(
                            [_rope(blk[:, t * V7X_LANES:(t + 1) * V7X_LANES], cos, s_lo, s_hi)
                             for t in range(hw // V7X_LANES)], axis=-1)
                    if sect == 0:
                        blk = blk * q_scale
                    o_ref[bi, n * heads_per_chunk + hh] = blk.astype(o_ref.dtype)

    if n_sect == 1:
        body(first_sect)
    else:
        sect_id = pl.program_id(1) // sect_blocks
        for s in range(first_sect, first_sect + n_sect):
            pl.when(sect_id == s - first_sect)(functools.partial(body, s))


def _mm_heads(x, w, layer, first_sect, n_sect, batch, hw, out_dtype, q_scale, rope_tabs=None):
    M, K = x.shape
    S = M // batch
    bm = _tile(M, 1024, 8)
    bn = _tile(K, 1024)
    sect_blocks = K // bn
    if bm >= S:
        bb, sr = bm // S, S
    else:
        bb, sr = 1, bm
    sblocks = S // sr
    joff = first_sect * sect_blocks
    nh = bn // hw
    in_specs = [pl.BlockSpec((bm, K), lambda i, j: (i, 0)),
                pl.BlockSpec((None, K, bn), lambda i, j: (layer, 0, joff + j))]
    args = [x, w]
    if rope_tabs is not None:
        for t in rope_tabs:
            in_specs.append(pl.BlockSpec((sr, V7X_LANES), lambda i, j: (i % sblocks, 0)))
            args.append(t)
    osz = jnp.dtype(out_dtype).itemsize
    need = 2 * (bm * K * 2 + K * bn * 2 + bm * bn * osz) + 6 * bm * V7X_MXU_COLS * 4
    return pl.pallas_call(
        functools.partial(_mm_heads_kernel, hw=hw, sect_blocks=sect_blocks, first_sect=first_sect, n_sect=n_sect,
                          rope=rope_tabs is not None, q_scale=q_scale),
        grid=(M // bm, n_sect * sect_blocks),
        in_specs=in_specs,
        out_specs=pl.BlockSpec((bb, nh, sr, hw), lambda i, j: (i // sblocks, j, i % sblocks, 0)),
        out_shape=jax.ShapeDtypeStruct((batch, n_sect * K // hw, S, hw), out_dtype),
        compiler_params=_cparams(("parallel", "arbitrary"), need),
        name="proj_heads",
    )(*args)


def _mm_resid_kernel(x_ref, w_ref, r_ref, g_ref, o_ref):
    bn = o_ref.shape[1]
    cw = min(bn, V7X_MXU_COLS)
    for n in range(bn // cw):
        cols = slice(n * cw, (n + 1) * cw)
        o_ref[:, cols] = r_ref[:, cols] + g_ref[0, 0, :, cols] * _dot(x_ref[...], w_ref[:, cols])


def _mm_resid(x, w, layer, res, mod, chunk, row_of_tile, bm_want, bn_want):
    M, K = x.shape
    N = w.shape[-1]
    bm = _tile(M, bm_want, 8)
    bn = _tile(N, bn_want)
    gblk = chunk * (N // bn)
    need = 2 * (bm * K * 2 + K * bn * 2 + 2 * bm * bn * 4) + 4 * bm * V7X_MXU_COLS * 4
    return pl.pallas_call(
        _mm_resid_kernel,
        grid=(M // bm, N // bn),
        in_specs=[pl.BlockSpec((bm, K), lambda i, j: (i, 0)),
                  pl.BlockSpec((None, K, bn), lambda i, j: (layer, 0, j)),
                  pl.BlockSpec((bm, bn), lambda i, j: (i, j)),
                  pl.BlockSpec((1, 1, 1, bn), lambda i, j: (layer, row_of_tile(i, bm), 0, gblk + j))],
        out_specs=pl.BlockSpec((bm, bn), lambda i, j: (i, j)),
        out_shape=jax.ShapeDtypeStruct((M, N), F32),
        compiler_params=_cparams(("parallel", "arbitrary"), need),
        name="proj_residual",
    )(x, w, res, mod)


def _upconv_kernel(x_ref, wa_ref, wb_ref, cwa_ref, cwb_ref, cba_ref, cbb_ref, o_ref, *w_out_refs, seq):
    if w_out_refs:
        @pl.when(pl.program_id(0) == 0)
        def _():
            w_out_refs[0][...] = wa_ref[...].astype(BF16)
            w_out_refs[1][...] = wb_ref[...].astype(BF16)

    x = x_ref[...]
    bm = x.shape[0]
    wa, wb = wa_ref[...].astype(BF16), wb_ref[...].astype(BF16)
    a = _dot(x, wa)
    b = _dot(x, wb)
    row = lax.broadcasted_iota(jnp.int32, a.shape, 0)
    pos = row & (seq - 1) if seq & (seq - 1) == 0 else lax.rem(row, jnp.int32(seq))
    first = pos == 0
    last = pos == seq - 1

    def conv(u, cw, cb):
        prev = jnp.where(first, 0.0, pltpu.roll(u, 1, 0))
        nxt = jnp.where(last, 0.0, pltpu.roll(u, bm - 1, 0))
        return prev * cw[0:1] + u * cw[1:2] + nxt * cw[2:3] + cb

    ua = conv(a, cwa_ref[...], cba_ref[...])
    ub = conv(b, cwb_ref[...], cbb_ref[...])
    o_ref[...] = (ua * jax.nn.sigmoid(ua) * ub).astype(o_ref.dtype)


def _upconv(h, w_up, conv_w, conv_b, layer, seq, w_halves=None):
    M, K = h.shape
    F = conv_w.shape[-1] // 2
    bm = _tile(M, max(2048, seq), seq)
    tf = _tile(F, V7X_MXU_COLS)
    nb = F // tf
    emit = w_halves is None
    wsz = 4 if emit else 2
    need = bm * K * 2 + 2 * (2 * K * tf * wsz + bm * tf * 2) + 10 * bm * tf * 4 + (6 * K * tf * 2 if emit else 0)
    if emit:
        w_args = (w_up, w_up)
        w_specs = [pl.BlockSpec((None, K, tf), lambda i, j: (layer, 0, j)),
                   pl.BlockSpec((None, K, tf), lambda i, j: (layer, 0, nb + j))]
    else:
        w_args = w_halves
        w_specs = [pl.BlockSpec((K, tf), lambda i, j: (0, j))] * 2
    out_specs = [pl.BlockSpec((bm, tf), lambda i, j: (i, j))]
    out_shape = [jax.ShapeDtypeStruct((M, F), BF16)]
    if emit:
        out_specs += [pl.BlockSpec((K, tf), lambda i, j: (0, jnp.where(i == 0, j, nb - 1)))] * 2
        out_shape += [jax.ShapeDtypeStruct((K, F), BF16)] * 2
    outs = pl.pallas_call(
        functools.partial(_upconv_kernel, seq=seq),
        grid=(M // bm, nb),
        in_specs=[pl.BlockSpec((bm, K), lambda i, j: (i, 0), pipeline_mode=pl.Buffered(1)), *w_specs,
                  pl.BlockSpec((None, 3, tf), lambda i, j: (layer, 0, j)),
                  pl.BlockSpec((None, 3, tf), lambda i, j: (layer, 0, nb + j)),
                  pl.BlockSpec((None, 1, tf), lambda i, j: (layer, 0, j)),
                  pl.BlockSpec((None, 1, tf), lambda i, j: (layer, 0, nb + j))],
        out_specs=out_specs,
        out_shape=out_shape,
        compiler_params=_cparams(("arbitrary", "arbitrary"), need),
        name="ffn_up_conv_gate",
    )(h, *w_args, conv_w, conv_w, conv_b, conv_b)
    return (outs[0], (outs[1], outs[2])) if emit else outs[0]


def _softmax_parts(parts):
    m = parts[0].max(axis=-1, keepdims=True)
    for p in parts[1:]:
        m = jnp.maximum(m, p.max(axis=-1, keepdims=True))
    es = [jnp.exp(p - m) for p in parts]
    l = es[0].sum(axis=-1, keepdims=True)
    for e in es[1:]:
        l = l + e.sum(axis=-1, keepdims=True)
    return es, l


def _diff_lambda(p, lam_init):
    t1 = jnp.sum(p[0:1] * p[1:2], axis=-1, keepdims=True)
    t2 = jnp.sum(p[2:3] * p[3:4], axis=-1, keepdims=True)
    return jnp.exp(t1) - jnp.exp(t2) + lam_init


def _head_norm(o, g, lam_init):
    y = o * lax.rsqrt(jnp.mean(o * o, axis=-1, keepdims=True) + DIFF_NORM_EPS)
    return y * g * (1.0 - lam_init)


def _ctx_dense_kernel(q_ref, k_ref, v_ref, o_ref):
    hb, d = q_ref.shape[1], q_ref.shape[3]
    sm = [_softmax_parts([s]) for s in [_dot_nt(q_ref[0, hh], k_ref[0, hh].astype(BF16)) for hh in range(hb)]]
    for hh, ((e,), l) in enumerate(sm):
        o = _dot(e.astype(BF16), v_ref[0, hh].astype(BF16)) / l
        o_ref[0, :, hh * d:(hh + 1) * d] = o.astype(o_ref.dtype)


def _ctx_dense_attention(q, k, v):
    B, H, S, d = q.shape
    hb = _tile(H, 8, 1)
    spec = pl.BlockSpec((1, hb, S, d), lambda b, h: (b, h, 0, 0))
    need = 2 * hb * S * d * (2 + 4 + 4 + 2) + 6 * S * S * 4
    return pl.pallas_call(
        _ctx_dense_kernel,
        grid=(B, H // hb),
        in_specs=[spec, spec, spec],
        out_specs=pl.BlockSpec((1, S, hb * d), lambda b, h: (b, 0, h)),
        out_shape=jax.ShapeDtypeStruct((B, S, H * d), BF16),
        compiler_params=_cparams(("parallel", "parallel"), need),
        name="ctx_dense_attention",
    )(q, k, v)


def _ctx_diff_kernel(lam_ref, g_ref, q_ref, k_ref, v_ref, o_ref, *, lam_init):
    hb, dv = q_ref.shape[1], v_ref.shape[3]
    dq = lam_ref.shape[-1]
    lam = _diff_lambda(lam_ref[0], lam_init)
    g = g_ref[...]
    scores = [_dot_nt(q_ref[0, hh, :, c * dq:(c + 1) * dq], k_ref[0, hh, :, c * dq:(c + 1) * dq].astype(BF16))
              for hh in range(hb) for c in range(2)]
    sm = [_softmax_parts([s]) for s in scores]
    for hh in range(hb):
        v = v_ref[0, hh].astype(BF16)
        os = [_dot(e.astype(BF16), v) / l for (e,), l in sm[2 * hh:2 * hh + 2]]
        o_ref[0, :, hh * dv:(hh + 1) * dv] = _head_norm(os[0] - lam * os[1], g, lam_init).astype(o_ref.dtype)


def _ctx_diff_attention(q, k, v, lam_params, subln, idx, lam_init):
    B, H, S, dv = v.shape
    dq = lam_params.shape[-1]
    hb = _tile(H, 4, 1)
    spec = pl.BlockSpec((1, hb, S, dv), lambda b, h: (b, h, 0, 0))
    need = 2 * hb * S * dv * (2 + 4 + 4 + 2) + 10 * S * S * 4
    return pl.pallas_call(
        functools.partial(_ctx_diff_kernel, lam_init=lam_init),
        grid=(B, H // hb),
        in_specs=[pl.BlockSpec((1, 4, dq), lambda b, h: (idx, 0, 0)),
                  pl.BlockSpec((None, 1, dv), lambda b, h: (idx, 0, 0)),
                  spec, spec, spec],
        out_specs=pl.BlockSpec((1, S, hb * dv), lambda b, h: (b, 0, h)),
        out_shape=jax.ShapeDtypeStruct((B, S, H * dv), BF16),
        compiler_params=_cparams(("parallel", "parallel"), need),
        name="ctx_diff_attention",
    )(lam_params, subln, q, k, v)


def _na_tables(rows, win_r, win_c):
    G, W = NA_GROUP_ROWS, GRID_W
    wr = min(win_r, rows)
    kr = min(rows, wr + G)
    ngroups = rows // G
    assert rows % G == 0 and kr % 2 == 0 and 2 * W == V7X_LANES and 2 * win_c - 1 <= W

    def block(g):
        ks = int(np.clip(G * g - wr // 2, 0, rows - kr))
        qr = G * g + np.arange(G)
        rs = np.clip(qr - wr // 2, 0, rows - wr)
        krow = ks + np.arange(kr)
        ok = (krow[None, :] >= rs[:, None]) & (krow[None, :] < rs[:, None] + wr)
        dr = np.where(ok, krow[None, :] - qr[:, None] + win_r - 1, -1)
        assert (ok.sum(1) == wr).all() and dr.max() <= 2 * win_r - 2 and (dr[ok] >= 0).all()
        return ks, dr

    reps = sorted({0, min(1, ngroups - 1), ngroups - 1})
    pats = [block(g)[1] for g in reps]
    starts, pat_of = [], []
    for g in range(ngroups):
        ks, dr = block(g)
        pid = 0 if g == 0 else (len(reps) - 1 if g == ngroups - 1 else 1)
        assert np.array_equal(dr, pats[pid])
        starts.append(ks)
        pat_of.append(pid)
    qc, kc = np.arange(W)[:, None], np.arange(W)[None, :]
    cs = np.clip(qc - win_c // 2, 0, W - win_c)
    col_ok = (kc >= cs) & (kc < cs + win_c)
    assert (np.abs(kc - qc)[col_ok] <= win_c - 1).all()
    colneg = np.where(col_ok, 0.0, NEG_INF).astype(np.float32)
    return kr, tuple(starts), tuple(pat_of), pats, np.concatenate([colneg, colneg], axis=1)


def _na_build_bias(rp_ref, colneg_ref, bias_ref, pats, win_c):
    W = GRID_W
    colneg = colneg_ref[...]
    lane = lax.broadcasted_iota(jnp.int32, (W, V7X_LANES), 1)
    cache = {}

    def tile(d_lo, d_hi):
        if (d_lo, d_hi) not in cache:
            if d_lo < 0 and d_hi < 0:
                val = jnp.full((W, V7X_LANES), NEG_INF, F32)
            else:
                src = None
                if d_lo >= 0:
                    src = jnp.broadcast_to(rp_ref[0, d_lo:d_lo + 1, :], (W, V7X_LANES))
                if d_hi >= 0:
                    hi = pltpu.roll(jnp.broadcast_to(rp_ref[0, d_hi:d_hi + 1, :], (W, V7X_LANES)), W, 1)
                    src = hi if src is None else src + hi
                val = pltpu.roll(src, V7X_LANES - (win_c - 1), 1, stride=1, stride_axis=0) + colneg
                if d_lo < 0:
                    val = jnp.where(lane < W, NEG_INF, val)
                if d_hi < 0:
                    val = jnp.where(lane >= W, NEG_INF, val)
            cache[(d_lo, d_hi)] = val
        return cache[(d_lo, d_hi)]

    for p, dr in enumerate(pats):
        for qr in range(dr.shape[0]):
            for m in range(dr.shape[1] // 2):
                bias_ref[p, qr * W:(qr + 1) * W, m * V7X_LANES:(m + 1) * V7X_LANES] = tile(
                    int(dr[qr, 2 * m]), int(dr[qr, 2 * m + 1]))


def _na_kernel(q_ref, k_ref, v_ref, kc_ref, vc_ref, rp_ref, colneg_ref, o_ref, bias_ref, *, starts, pat_of, pats, kr,
               win_c):
    @pl.when(pl.program_id(1) == 0)
    def _():
        _na_build_bias(rp_ref, colneg_ref, bias_ref, pats, win_c)

    gq = NA_GROUP_ROWS * GRID_W
    kc = kc_ref[0, 0].astype(BF16)
    vc = vc_ref[0, 0].astype(BF16)
    scores = []
    for g, (ks, pid) in enumerate(zip(starts, pat_of)):
        q = q_ref[0, 0, g * gq:(g + 1) * gq]
        kw = k_ref[0, 0, ks * GRID_W:(ks + kr) * GRID_W]
        scores.append([_dot_nt(q, kw) + bias_ref[pid], _dot_nt(q, kc)])
    sm = [_softmax_parts(s) for s in scores]
    for g, (ks, (es, l)) in enumerate(zip(starts, sm)):
        vw = v_ref[0, 0, ks * GRID_W:(ks + kr) * GRID_W]
        o = (_dot(es[0].astype(BF16), vw) + _dot(es[1].astype(BF16), vc)) / l
        o_ref[0, g * gq:(g + 1) * gq, :] = o.astype(o_ref.dtype)


def _na_attention(qkv, k_ctx, v_ctx, rpb):
    B, H3, N, d = qkv.shape
    H = H3 // 3
    P = k_ctx.shape[2]
    rows = N // GRID_W
    win_r, win_c = (rpb.shape[1] + 1) // 2, (rpb.shape[2] + 1) // 2
    kr, starts, pat_of, pats, colneg = _na_tables(rows, win_r, win_c)
    rp = jnp.pad(rpb.astype(F32), ((0, 0), (0, -rpb.shape[1] % 8), (0, V7X_LANES - rpb.shape[2])))
    gq, gk = NA_GROUP_ROWS * GRID_W, kr * GRID_W
    need = (2 * (3 * N * d * 2 + 2 * P * d * 4 + N * d * 2) + len(pats) * gq * gk * 4
            + 3 * len(starts) * gq * (gk + P) * 4)
    return pl.pallas_call(
        functools.partial(_na_kernel, starts=starts, pat_of=pat_of, pats=pats, kr=kr, win_c=win_c),
        grid=(H, B),
        in_specs=[pl.BlockSpec((1, 1, N, d), lambda h, b: (b, h, 0, 0)),
                  pl.BlockSpec((1, 1, N, d), lambda h, b: (b, H + h, 0, 0)),
                  pl.BlockSpec((1, 1, N, d), lambda h, b: (b, 2 * H + h, 0, 0)),
                  pl.BlockSpec((1, 1, P, d), lambda h, b: (b, h, 0, 0)),
                  pl.BlockSpec((1, 1, P, d), lambda h, b: (b, h, 0, 0)),
                  pl.BlockSpec((1,) + rp.shape[1:], lambda h, b: (h, 0, 0)),
                  pl.BlockSpec(colneg.shape, lambda h, b: (0, 0))],
        out_specs=pl.BlockSpec((1, N, d), lambda h, b: (b, 0, h)),
        out_shape=jax.ShapeDtypeStruct((B, N, H * d), BF16),
        scratch_shapes=[pltpu.VMEM((len(pats), gq, gk), F32)],
        compiler_params=_cparams(("parallel", "arbitrary"), need),
        name="neighbourhood_attention",
    )(qkv, qkv, qkv, k_ctx, v_ctx, rp, jnp.asarray(colneg))


def _lat_diff_kernel(lam_ref, g_ref, q_ref, k_ref, v_ref, kc_ref, vc_ref, o_ref, *, lam_init):
    dq = lam_ref.shape[-1]
    tq = q_ref.shape[2]
    sub = min(tq, ATTN_ROWS)
    lam = _diff_lambda(lam_ref[0], lam_init)
    v = v_ref[0, 0]
    vc = vc_ref[0, 0].astype(BF16)
    scores = []
    for t in range(tq // sub):
        for c in range(2):
            cols = slice(c * dq, (c + 1) * dq)
            q = q_ref[0, 0, t * sub:(t + 1) * sub, cols]
            scores.append([_dot_nt(q, k_ref[0, 0, :, cols]), _dot_nt(q, kc_ref[0, 0, :, cols].astype(BF16))])
    sm = [_softmax_parts(s) for s in scores]
    os = [(_dot(es[0].astype(BF16), v) + _dot(es[1].astype(BF16), vc)) / l for es, l in sm]
    for t in range(tq // sub):
        o = os[2 * t] - lam * os[2 * t + 1]
        o_ref[0, t * sub:(t + 1) * sub, :] = _head_norm(o, g_ref[...], lam_init).astype(o_ref.dtype)


def _lat_diff_attention(qkv, k_ctx, v_ctx, lam_params, subln, idx, lam_init):
    B, H3, N, dv = qkv.shape
    H = H3 // 3
    P = k_ctx.shape[2]
    dq = lam_params.shape[-1]
    tq = _tile(N, 4 * ATTN_ROWS, 8)
    need = 2 * (tq * dv * 4 + 2 * N * dv * 2 + 2 * P * dv * 4) + 8 * tq * (N + P) * 4
    return pl.pallas_call(
        functools.partial(_lat_diff_kernel, lam_init=lam_init),
        grid=(B, H, N // tq),
        in_specs=[pl.BlockSpec((1, 4, dq), lambda b, h, t: (idx, 0, 0)),
                  pl.BlockSpec((None, 1, dv), lambda b, h, t: (idx, 0, 0)),
                  pl.BlockSpec((1, 1, tq, dv), lambda b, h, t: (b, h, t, 0)),
                  pl.BlockSpec((1, 1, N, dv), lambda b, h, t: (b, H + h, 0, 0)),
                  pl.BlockSpec((1, 1, N, dv), lambda b, h, t: (b, 2 * H + h, 0, 0)),
                  pl.BlockSpec((1, 1, P, dv), lambda b, h, t: (b, h, 0, 0)),
                  pl.BlockSpec((1, 1, P, dv), lambda b, h, t: (b, h, 0, 0))],
        out_specs=pl.BlockSpec((1, tq, dv), lambda b, h, t: (b, t, h)),
        out_shape=jax.ShapeDtypeStruct((B, N, H * dv), BF16),
        compiler_params=_cparams(("parallel", "parallel", "arbitrary"), need),
        name="latent_diff_attention",
    )(lam_params, subln, qkv, qkv, qkv, k_ctx, v_ctx)


def _rope_tables(n_lat, dq):
    half = dq // 2
    nf = half // 2
    t = jnp.arange(n_lat)
    pos_r = (t // GRID_W).astype(F32)
    pos_c = (t % GRID_W).astype(F32)
    inv = ROPE_THETA ** (-jnp.arange(nf, dtype=F32) / nf)
    zeros = jnp.zeros((n_lat, nf), F32)

    def tabs(pos):
        ang = pos[:, None] * inv[None, :]
        cos, sin = jnp.cos(ang), jnp.sin(ang)
        return (jnp.concatenate([cos, cos], -1), jnp.concatenate([-sin, zeros], -1),
                jnp.concatenate([zeros, sin], -1))

    r, c = tabs(pos_r), tabs(pos_c)
    return tuple(jnp.concatenate([a, b], -1) for a, b in zip(r, c))


def kernel(x_prompt, x_sample, cache_k_0, cache_v_0, cache_k_1, cache_v_1, cache_k_2, cache_v_2, cache_k_3, cache_v_3, c, c_ctx, norm_attn, norm_ffn, w_ada, b_ada, w_qkv, w_o, na_rpb, diff_lambda, diff_subln, w_up, conv_w, conv_b, w_down, final_norm):
    caches = [(cache_k_0, cache_v_0), (cache_k_1, cache_v_1), (cache_k_2, cache_v_2), (cache_k_3, cache_v_3)]
    Bp, Sp, D = x_prompt.shape
    Bs, Ns, _ = x_sample.shape
    L = w_qkv.shape[0]
    Hn = na_rpb.shape[1]
    dn = D // Hn
    dq = diff_lambda.shape[-1]
    dv = 2 * dq
    Hd = D // dv
    assert dq == V7X_LANES and dn % V7X_LANES == 0

    n_cond = 1 + Bs
    cond = jnp.concatenate([c_ctx[None, :], c, jnp.zeros((-n_cond % 16, D), F32)], axis=0)
    mod = _ada_params(cond, w_ada, b_ada).reshape(L, cond.shape[0], 1, 6 * D)

    w_qkv_b, w_o_b = w_qkv.astype(BF16), w_o.astype(BF16)
    w_down_b = w_down.astype(BF16)
    conv_b3 = conv_b.reshape(L, 1, conv_b.shape[-1])
    diff_subln = diff_subln.reshape(diff_subln.shape[0], 1, dv)
    rope_tabs = _rope_tables(Ns, dq)

    ctx_row = lambda b: 0
    ctx_tile_row = lambda i, bm: 0
    lat_row = lambda b: 1 + b
    lat_tile_row = lambda i, bm: 1 + (i * bm) // Ns

    xp = x_prompt.reshape(Bp * Sp, D)
    xs = x_sample.reshape(Bs * Ns, D)
    new_state = []
    for l in range(L):
        is_na = l % 2 == 0
        hw = dn if is_na else dv
        H = Hn if is_na else Hd
        q_scale = (dn if is_na else dq) ** -0.5
        lam_init = 0.8 - 0.6 * math.exp(-0.3 * l)

        h = _rms_norm(xp.reshape(Bp, Sp, D), norm_attn[l], BF16, mod, l, (0, 1), ctx_row).reshape(Bp * Sp, D)
        q = _mm_heads(h, w_qkv_b, l, 0, 1, Bp, hw, BF16, q_scale)
        k_new = _mm_heads(h, w_qkv_b, l, 1, 1, Bp, hw, F32, q_scale)
        v_new = _mm_heads(h, w_qkv_b, l, 2, 1, Bp, hw, F32, q_scale)
        if is_na:
            o = _ctx_dense_attention(q, k_new, v_new)
            new_state += [k_new, v_new]
        else:
            o = _ctx_diff_attention(q, k_new, v_new, diff_lambda, diff_subln, l // 2, lam_init)
            new_state += [k_new.reshape(Bp, H, Sp, 2, dq), v_new]
        xp = _mm_resid(o.reshape(Bp * Sp, D), w_o_b, l, xp, mod, 2, ctx_tile_row, 1024, 1024)
        h = _rms_norm(xp.reshape(Bp, Sp, D), norm_ffn[l], BF16, mod, l, (3, 4), ctx_row).reshape(Bp * Sp, D)
        u, w_up_halves = _upconv(h, w_up, conv_w, conv_b3, l, Sp)
        xp = _mm_resid(u, w_down_b, l, xp, mod, 5, ctx_tile_row, 512, 512)

        k_ctx, v_ctx = caches[l]
        h = _rms_norm(xs.reshape(Bs, Ns, D), norm_attn[l], BF16, mod, l, (0, 1), lat_row).reshape(Bs * Ns, D)
        if is_na:
            qkv = _mm_heads(h, w_qkv_b, l, 0, 3, Bs, hw, BF16, q_scale)
            o = _na_attention(qkv, k_ctx, v_ctx, na_rpb[l // 2])
        else:
            qkv = _mm_heads(h, w_qkv_b, l, 0, 3, Bs, hw, BF16, q_scale, rope_tabs)
            k_ctx = k_ctx.reshape(k_ctx.shape[0], H, k_ctx.shape[2], dv)
            o = _lat_diff_attention(qkv, k_ctx, v_ctx, diff_lambda, diff_subln, l // 2, lam_init)
        xs = _mm_resid(o.reshape(Bs * Ns, D), w_o_b, l, xs, mod, 2, lat_tile_row, 1024, 1024)
        h = _rms_norm(xs.reshape(Bs, Ns, D), norm_ffn[l], BF16, mod, l, (3, 4), lat_row).reshape(Bs * Ns, D)
        u = _upconv(h, None, conv_w, conv_b3, l, Ns, w_up_halves)
        xs = _mm_resid(u, w_down_b, l, xs, mod, 5, lat_tile_row, 512, 512)

    y_prompt = _rms_norm(xp.reshape(Bp, Sp, D), final_norm, F32)
    y_sample = _rms_norm(xs.reshape(Bs, Ns, D), final_norm, F32)
    return (y_prompt, y_sample, *new_state)
```

```python
import functools
import math

import numpy as np
import jax
import jax.numpy as jnp
from jax import lax
from jax.experimental import pallas as pl
from jax.experimental.pallas import tpu as pltpu

GRID_W = 64
ROPE_THETA = 10000.0
NORM_EPS = 1e-6
DIFF_NORM_EPS = 1e-5
NEG_INF = -1e30
NA_GROUP_ROWS = 4
V7X_VMEM_BYTES = 64 * 1024 * 1024
V7X_VMEM_RESERVE = 6 * 1024 * 1024
V7X_LANES = 128
V7X_MXU_COLS = 256
ATTN_ROWS = 256
FFN_MM_ROWS = 512
FFN_EPI_ROWS = 64
CONV_PAD = 8
CONV_LAG = 16
F32 = jnp.float32
BF16 = jnp.bfloat16


def _cparams(semantics, need_bytes):
    limit = int(min(V7X_VMEM_BYTES - V7X_VMEM_RESERVE, max(need_bytes, 16 * 1024 * 1024)))
    return pltpu.CompilerParams(dimension_semantics=semantics, vmem_limit_bytes=limit)


def _tile(n, want, quantum=V7X_LANES):
    if n <= want:
        return n
    t = (want // quantum) * quantum
    while t >= quantum:
        if n % t == 0:
            return t
        t -= quantum
    return n


def _dot(a, b):
    return jnp.dot(a, b, preferred_element_type=F32)


def _dot_nt(a, b):
    return lax.dot_general(a, b, (((1,), (1,)), ((), ())), preferred_element_type=F32)


def _ada_kernel(c_ref, w_ref, b_ref, o_ref):
    c = c_ref[...]
    a = (c * jax.nn.sigmoid(c)).astype(BF16)
    o_ref[0] = _dot(a, w_ref[0].astype(BF16)) + b_ref[0]


def _ada_params(cond, w_ada, b_ada):
    L, D, N = w_ada.shape
    R = cond.shape[0]
    tn = _tile(N, 512)
    need = 2 * (D * tn * 4 + R * tn * 8) + 2 * R * D * 4 + D * tn * 2
    return pl.pallas_call(
        _ada_kernel,
        grid=(L, N // tn),
        in_specs=[pl.BlockSpec((R, D), lambda l, j: (0, 0)),
                  pl.BlockSpec((1, D, tn), lambda l, j: (l, 0, j)),
                  pl.BlockSpec((1, 1, tn), lambda l, j: (l, 0, j))],
        out_specs=pl.BlockSpec((1, R, tn), lambda l, j: (l, 0, j)),
        out_shape=jax.ShapeDtypeStruct((L, R, N), F32),
        compiler_params=_cparams(("parallel", "parallel"), need),
        name="ada_params",
    )(cond, w_ada, b_ada.reshape(L, 1, N))


def _norm_kernel(x_ref, g_ref, *rest, modulate, eps):
    o_ref = rest[-1]
    x = x_ref[0].astype(F32)
    ms = jnp.mean(x * x, axis=-1, keepdims=True)
    y = x * lax.rsqrt(ms + eps) * g_ref[...]
    if modulate:
        sh_ref, sc_ref = rest[0], rest[1]
        y = y * (1.0 + sc_ref[0, 0]) + sh_ref[0, 0]
    o_ref[0] = y.astype(o_ref.dtype)


def _rms_norm(x, g, out_dtype, mod=None, layer=0, chunks=None, row_of_batch=None):
    B, S, D = x.shape
    ts = _tile(S, 512, 8)
    in_specs = [pl.BlockSpec((1, ts, D), lambda b, s: (b, s, 0)),
                pl.BlockSpec((1, D), lambda b, s: (0, 0))]
    args = [x, g.reshape(1, D)]
    if mod is not None:
        for ch in chunks:
            in_specs.append(pl.BlockSpec((1, 1, 1, D), lambda b, s, ch=ch: (layer, row_of_batch(b), 0, ch)))
            args.append(mod)
    need = 2 * ts * D * (4 + jnp.dtype(out_dtype).itemsize) + 4 * ts * D * 4
    return pl.pallas_call(
        functools.partial(_norm_kernel, modulate=mod is not None, eps=NORM_EPS),
        grid=(B, S // ts),
        in_specs=in_specs,
        out_specs=pl.BlockSpec((1, ts, D), lambda b, s: (b, s, 0)),
        out_shape=jax.ShapeDtypeStruct((B, S, D), out_dtype),
        compiler_params=_cparams(("parallel", "parallel"), need),
        name="rms_norm",
    )(*args)


def _rope(x, cos, sin_lo, sin_hi):
    up = pltpu.roll(x, 96, 1)
    dn = pltpu.roll(x, 32, 1)
    return x * cos + up * sin_lo + dn * sin_hi


def _mm_heads_kernel(x_ref, w_ref, *rest, hw, sect_blocks, first_sect, n_sect, rope, q_scale):
    o_ref = rest[-1]
    bb, nh, sr, _ = o_ref.shape
    cw = max(hw, V7X_MXU_COLS)
    heads_per_chunk = cw // hw

    def body(sect):
        for n in range(nh // heads_per_chunk):
            acc = _dot(x_ref[...], w_ref[:, n * cw:(n + 1) * cw])
            for hh in range(heads_per_chunk):
                for bi in range(bb):
                    blk = acc[bi * sr:(bi + 1) * sr, hh * hw:(hh + 1) * hw]
                    if rope and sect < 2:
                        cos, s_lo, s_hi = rest[0][...], rest[1][...], rest[2][...]
                        blk = jnp.concatenate(
                            [_rope(blk[:, t * V7X_LANES:(t + 1) * V7X_LANES], cos, s_lo, s_hi)
                             for t in range(hw // V7X_LANES)], axis=-1)
                    if sect == 0:
                        blk = blk * q_scale
                    o_ref[bi, n * heads_per_chunk + hh] = blk.astype(o_ref.dtype)

    if n_sect == 1:
        body(first_sect)
    else:
        sect_id = pl.program_id(1) // sect_blocks
        for s in range(first_sect, first_sect + n_sect):
            pl.when(sect_id == s - first_sect)(functools.partial(body, s))


def _mm_heads(x, w, layer, first_sect, n_sect, batch, hw, out_dtype, q_scale, rope_tabs=None):
    M, K = x.shape
    S = M // batch
    bm = _tile(M, 1024, 8)
    bn = _tile(K, 1024)
    sect_blocks = K // bn
    if bm >= S:
        bb, sr = bm // S, S
    else:
        bb, sr = 1, bm
    sblocks = S // sr
    joff = first_sect * sect_blocks
    nh = bn // hw
    in_specs = [pl.BlockSpec((bm, K), lambda i, j: (i, 0)),
                pl.BlockSpec((None, K, bn), lambda i, j: (layer, 0, joff + j))]
    args = [x, w]
    if rope_tabs is not None:
        for t in rope_tabs:
            in_specs.append(pl.BlockSpec((sr, V7X_LANES), lambda i, j: (i % sblocks, 0)))
            args.append(t)
    osz = jnp.dtype(out_dtype).itemsize
    need = 2 * (bm * K * 2 + K * bn * 2 + bm * bn * osz) + 6 * bm * V7X_MXU_COLS * 4
    return pl.pallas_call(
        functools.partial(_mm_heads_kernel, hw=hw, sect_blocks=sect_blocks, first_sect=first_sect, n_sect=n_sect,
                          rope=rope_tabs is not None, q_scale=q_scale),
        grid=(M // bm, n_sect * sect_blocks),
        in_specs=in_specs,
        out_specs=pl.BlockSpec((bb, nh, sr, hw), lambda i, j: (i // sblocks, j, i % sblocks, 0)),
        out_shape=jax.ShapeDtypeStruct((batch, n_sect * K // hw, S, hw), out_dtype),
        compiler_params=_cparams(("parallel", "arbitrary"), need),
        name="proj_heads",
    )(*args)


def _mm_resid_kernel(x_ref, w_ref, r_ref, g_ref, o_ref):
    bn = o_ref.shape[1]
    cw = min(bn, V7X_MXU_COLS)
    for n in range(bn // cw):
        cols = slice(n * cw, (n + 1) * cw)
        o_ref[:, cols] = r_ref[:, cols] + g_ref[0, 0, :, cols] * _dot(x_ref[...], w_ref[:, cols])


def _mm_resid(x, w, layer, res, mod, chunk, row_of_tile, bm_want, bn_want):
    M, K = x.shape
    N = w.shape[-1]
    bm = _tile(M, bm_want, 8)
    bn = _tile(N, bn_want)
    gblk = chunk * (N // bn)
    need = 2 * (bm * K * 2 + K * bn * 2 + 2 * bm * bn * 4) + 4 * bm * V7X_MXU_COLS * 4
    return pl.pallas_call(
        _mm_resid_kernel,
        grid=(M // bm, N // bn),
        in_specs=[pl.BlockSpec((bm, K), lambda i, j: (i, 0)),
                  pl.BlockSpec((None, K, bn), lambda i, j: (layer, 0, j)),
                  pl.BlockSpec((bm, bn), lambda i, j: (i, j)),
                  pl.BlockSpec((1, 1, 1, bn), lambda i, j: (layer, row_of_tile(i, bm), 0, gblk + j))],
        out_specs=pl.BlockSpec((bm, bn), lambda i, j: (i, j)),
        out_shape=jax.ShapeDtypeStruct((M, N), F32),
        compiler_params=_cparams(("parallel", "arbitrary"), need),
        name="proj_residual",
    )(x, w, res, mod)


def _upconv_kernel(z_ref, x_ref, wa_ref, wb_ref, cwa_ref, cwb_ref, cba_ref, cbb_ref, o_ref, *rest, seq):
    w_out_refs, (a_ref, b_ref) = rest[:-2], rest[-2:]
    if w_out_refs:
        @pl.when(pl.program_id(0) == 0)
        def _():
            w_out_refs[0][...] = wa_ref[...].astype(BF16)
            w_out_refs[1][...] = wb_ref[...].astype(BF16)

    bm, K = x_ref.shape
    tf = o_ref.shape[1]
    pad, lag = CONV_PAD, CONV_LAG
    zero = z_ref[...]
    cwa, cwb, cba, cbb = cwa_ref[...], cwb_ref[...], cba_ref[...], cbb_ref[...]
    for ref in (a_ref, b_ref):
        ref[0:pad, :] = jnp.zeros((pad, tf), F32)
        ref[pad + bm:2 * pad + bm, :] = jnp.zeros((pad, tf), F32)

    def conv(ref, r0, n, cw, cb, first=None, last=None):
        prev, nxt = ref[pad + r0 - 1:pad + r0 - 1 + n, :], ref[pad + r0 + 1:pad + r0 + 1 + n, :]
        if first is not None:
            prev, nxt = jnp.where(first, 0.0, prev), jnp.where(last, 0.0, nxt)
        return prev * cw[0:1] + ref[pad + r0:pad + r0 + n, :] * cw[1:2] + nxt * cw[2:3] + cb

    def gated(r0, n, first=None, last=None):
        ua, ub = conv(a_ref, r0, n, cwa, cba, first, last), conv(b_ref, r0, n, cwb, cbb, first, last)
        g = ua * jax.nn.sigmoid(ua) * ub
        o_ref[r0:r0 + n, :] = g.astype(o_ref.dtype)
        return g

    def tie_of(g):
        bits = pltpu.bitcast(g, jnp.int32).reshape(g.shape[0] // 8, 8, tf)
        r = bits[0]
        for t in range(1, bits.shape[0]):
            r = r | bits[t]
        folded = r[:, 0:V7X_LANES]
        for t in range(1, tf // V7X_LANES):
            folded = folded | r[:, t * V7X_LANES:(t + 1) * V7X_LANES]
        return folded & zero

    def tied(lhs, tie):
        head = pltpu.bitcast(pltpu.bitcast(lhs[0:16, 0:V7X_LANES], jnp.int32) | tie, BF16)
        top = jnp.concatenate([head, lhs[0:16, V7X_LANES:]], axis=1)
        return jnp.concatenate([top, lhs[16:]], axis=0)

    cm = min(bm, FFN_MM_ROWS)
    kc = min(K, V7X_MXU_COLS)
    nk = K // kc
    pending = []
    for r in range(bm // cm):
        pieces = [pending[t::nk] for t in range(nk)]
        tie, acc_a, acc_b = None, None, None
        for kt in range(nk):
            lhs = x_ref[r * cm:(r + 1) * cm, kt * kc:(kt + 1) * kc]
            if tie is not None:
                lhs = tied(lhs, tie)
            da = _dot(lhs, wa_ref[kt * kc:(kt + 1) * kc, :].astype(BF16))
            db = _dot(lhs, wb_ref[kt * kc:(kt + 1) * kc, :].astype(BF16))
            acc_a = da if acc_a is None else acc_a + da
            acc_b = db if acc_b is None else acc_b + db
            tie = None
            for r0, n in pieces[kt]:
                t = tie_of(gated(r0, n))
                tie = t if tie is None else tie | t
        a_ref[pad + r * cm:pad + (r + 1) * cm, :] = acc_a
        b_ref[pad + r * cm:pad + (r + 1) * cm, :] = acc_b
        lo = max(r * cm - lag, 0)
        hi = bm if (r + 1) * cm == bm else (r + 1) * cm - lag
        pending = [(r0, min(FFN_EPI_ROWS, hi - r0)) for r0 in range(lo, hi, FFN_EPI_ROWS)]
    for r0, n in pending:
        gated(r0, n)
    row = lax.broadcasted_iota(jnp.int32, (2 * lag, tf), 0)
    for rb in range(seq, bm, seq):
        gated(rb - lag, 2 * lag, first=row == lag, last=row == lag - 1)


def _upconv(h, w_up, conv_w, conv_b, layer, seq, w_halves=None):
    M, K = h.shape
    F = conv_w.shape[-1] // 2
    bm = _tile(M, max(2048, seq), seq)
    tf = _tile(F, V7X_MXU_COLS)
    nb = F // tf
    emit = w_halves is None
    wsz = 4 if emit else 2
    need = bm * K * 2 + 2 * (2 * K * tf * wsz + bm * tf * 2) + 10 * bm * tf * 4 + (6 * K * tf * 2 if emit else 0)
    if emit:
        w_args = (w_up, w_up)
        w_specs = [pl.BlockSpec((None, K, tf), lambda i, j: (layer, 0, j)),
                   pl.BlockSpec((None, K, tf), lambda i, j: (layer, 0, nb + j))]
    else:
        w_args = w_halves
        w_specs = [pl.BlockSpec((K, tf), lambda i, j: (0, j))] * 2
    out_specs = [pl.BlockSpec((bm, tf), lambda i, j: (i, j))]
    out_shape = [jax.ShapeDtypeStruct((M, F), BF16)]
    if emit:
        out_specs += [pl.BlockSpec((K, tf), lambda i, j: (0, jnp.where(i == 0, j, nb - 1)))] * 2
        out_shape += [jax.ShapeDtypeStruct((K, F), BF16)] * 2
    run_time_zero = jnp.zeros((8, V7X_LANES), jnp.int32)
    outs = pl.pallas_call(
        functools.partial(_upconv_kernel, seq=seq),
        grid=(M // bm, nb),
        in_specs=[pl.BlockSpec((8, V7X_LANES), lambda i, j: (0, 0)),
                  pl.BlockSpec((bm, K), lambda i, j: (i, 0), pipeline_mode=pl.Buffered(1)), *w_specs,
                  pl.BlockSpec((None, 3, tf), lambda i, j: (layer, 0, j)),
                  pl.BlockSpec((None, 3, tf), lambda i, j: (layer, 0, nb + j)),
                  pl.BlockSpec((None, 1, tf), lambda i, j: (layer, 0, j)),
                  pl.BlockSpec((None, 1, tf), lambda i, j: (layer, 0, nb + j))],
        out_specs=out_specs,
        out_shape=out_shape,
        scratch_shapes=[pltpu.VMEM((bm + 2 * CONV_PAD, tf), F32)] * 2,
        compiler_params=_cparams(("arbitrary", "arbitrary"), need),
        name="ffn_up_conv_gate",
    )(run_time_zero, h, *w_args, conv_w, conv_w, conv_b, conv_b)
    return (outs[0], (outs[1], outs[2])) if emit else outs[0]


def _softmax_parts(parts):
    m = parts[0].max(axis=-1, keepdims=True)
    for p in parts[1:]:
        m = jnp.maximum(m, p.max(axis=-1, keepdims=True))
    es = [jnp.exp(p - m) for p in parts]
    l = es[0].sum(axis=-1, keepdims=True)
    for e in es[1:]:
        l = l + e.sum(axis=-1, keepdims=True)
    return es, l


def _diff_lambda(p, lam_init):
    t1 = jnp.sum(p[0:1] * p[1:2], axis=-1, keepdims=True)
    t2 = jnp.sum(p[2:3] * p[3:4], axis=-1, keepdims=True)
    return jnp.exp(t1) - jnp.exp(t2) + lam_init


def _head_norm(o, g, lam_init):
    y = o * lax.rsqrt(jnp.mean(o * o, axis=-1, keepdims=True) + DIFF_NORM_EPS)
    return y * g * (1.0 - lam_init)


def _ctx_dense_kernel(q_ref, k_ref, v_ref, o_ref):
    hb, d = q_ref.shape[1], q_ref.shape[3]
    sm = [_softmax_parts([s]) for s in [_dot_nt(q_ref[0, hh], k_ref[0, hh].astype(BF16)) for hh in range(hb)]]
    for hh, ((e,), l) in enumerate(sm):
        o = _dot(e.astype(BF16), v_ref[0, hh].astype(BF16)) / l
        o_ref[0, :, hh * d:(hh + 1) * d] = o.astype(o_ref.dtype)


def _ctx_dense_attention(q, k, v):
    B, H, S, d = q.shape
    hb = _tile(H, 8, 1)
    spec = pl.BlockSpec((1, hb, S, d), lambda b, h: (b, h, 0, 0))
    need = 2 * hb * S * d * (2 + 4 + 4 + 2) + 6 * S * S * 4
    return pl.pallas_call(
        _ctx_dense_kernel,
        grid=(B, H // hb),
        in_specs=[spec, spec, spec],
        out_specs=pl.BlockSpec((1, S, hb * d), lambda b, h: (b, 0, h)),
        out_shape=jax.ShapeDtypeStruct((B, S, H * d), BF16),
        compiler_params=_cparams(("parallel", "parallel"), need),
        name="ctx_dense_attention",
    )(q, k, v)


def _ctx_diff_kernel(lam_ref, g_ref, q_ref, k_ref, v_ref, o_ref, *, lam_init):
    hb, dv = q_ref.shape[1], v_ref.shape[3]
    dq = lam_ref.shape[-1]
    lam = _diff_lambda(lam_ref[0], lam_init)
    g = g_ref[...]
    scores = [_dot_nt(q_ref[0, hh, :, c * dq:(c + 1) * dq], k_ref[0, hh, :, c * dq:(c + 1) * dq].astype(BF16))
              for hh in range(hb) for c in range(2)]
    sm = [_softmax_parts([s]) for s in scores]
    for hh in range(hb):
        v = v_ref[0, hh].astype(BF16)
        os = [_dot(e.astype(BF16), v) / l for (e,), l in sm[2 * hh:2 * hh + 2]]
        o_ref[0, :, hh * dv:(hh + 1) * dv] = _head_norm(os[0] - lam * os[1], g, lam_init).astype(o_ref.dtype)


def _ctx_diff_attention(q, k, v, lam_params, subln, idx, lam_init):
    B, H, S, dv = v.shape
    dq = lam_params.shape[-1]
    hb = _tile(H, 4, 1)
    spec = pl.BlockSpec((1, hb, S, dv), lambda b, h: (b, h, 0, 0))
    need = 2 * hb * S * dv * (2 + 4 + 4 + 2) + 10 * S * S * 4
    return pl.pallas_call(
        functools.partial(_ctx_diff_kernel, lam_init=lam_init),
        grid=(B, H // hb),
        in_specs=[pl.BlockSpec((1, 4, dq), lambda b, h: (idx, 0, 0)),
                  pl.BlockSpec((None, 1, dv), lambda b, h: (idx, 0, 0)),
                  spec, spec, spec],
        out_specs=pl.BlockSpec((1, S, hb * dv), lambda b, h: (b, 0, h)),
        out_shape=jax.ShapeDtypeStruct((B, S, H * dv), BF16),
        compiler_params=_cparams(("parallel", "parallel"), need),
        name="ctx_diff_attention",
    )(lam_params, subln, q, k, v)


def _na_tables(rows, win_r, win_c):
    G, W = NA_GROUP_ROWS, GRID_W
    wr = min(win_r, rows)
    kr = min(rows, wr + G)
    ngroups = rows // G
    assert rows % G == 0 and kr % 2 == 0 and 2 * W == V7X_LANES and 2 * win_c - 1 <= W

    def block(g):
        ks = int(np.clip(G * g - wr // 2, 0, rows - kr))
        qr = G * g + np.arange(G)
        rs = np.clip(qr - wr // 2, 0, rows - wr)
        krow = ks + np.arange(kr)
        ok = (krow[None, :] >= rs[:, None]) & (krow[None, :] < rs[:, None] + wr)
        dr = np.where(ok, krow[None, :] - qr[:, None] + win_r - 1, -1)
        assert (ok.sum(1) == wr).all() and dr.max() <= 2 * win_r - 2 and (dr[ok] >= 0).all()
        return ks, dr

    reps = sorted({0, min(1, ngroups - 1), ngroups - 1})
    pats = [block(g)[1] for g in reps]
    starts, pat_of = [], []
    for g in range(ngroups):
        ks, dr = block(g)
        pid = 0 if g == 0 else (len(reps) - 1 if g == ngroups - 1 else 1)
        assert np.array_equal(dr, pats[pid])
        starts.append(ks)
        pat_of.append(pid)
    qc, kc = np.arange(W)[:, None], np.arange(W)[None, :]
    cs = np.clip(qc - win_c // 2, 0, W - win_c)
    col_ok = (kc >= cs) & (kc < cs + win_c)
    assert (np.abs(kc - qc)[col_ok] <= win_c - 1).all()
    colneg = np.where(col_ok, 0.0, NEG_INF).astype(np.float32)
    return kr, tuple(starts), tuple(pat_of), pats, np.concatenate([colneg, colneg], axis=1)


def _na_build_bias(rp_ref, colneg_ref, bias_ref, pats, win_c):
    W = GRID_W
    colneg = colneg_ref[...]
    lane = lax.broadcasted_iota(jnp.int32, (W, V7X_LANES), 1)
    cache = {}

    def tile(d_lo, d_hi):
        if (d_lo, d_hi) not in cache:
            if d_lo < 0 and d_hi < 0:
                val = jnp.full((W, V7X_LANES), NEG_INF, F32)
            else:
                src = None
                if d_lo >= 0:
                    src = jnp.broadcast_to(rp_ref[0, d_lo:d_lo + 1, :], (W, V7X_LANES))
                if d_hi >= 0:
                    hi = pltpu.roll(jnp.broadcast_to(rp_ref[0, d_hi:d_hi + 1, :], (W, V7X_LANES)), W, 1)
                    src = hi if src is None else src + hi
                val = pltpu.roll(src, V7X_LANES - (win_c - 1), 1, stride=1, stride_axis=0) + colneg
                if d_lo < 0:
                    val = jnp.where(lane < W, NEG_INF, val)
                if d_hi < 0:
                    val = jnp.where(lane >= W, NEG_INF, val)
            cache[(d_lo, d_hi)] = val
        return cache[(d_lo, d_hi)]

    for p, dr in enumerate(pats):
        for qr in range(dr.shape[0]):
            for m in range(dr.shape[1] // 2):
                bias_ref[p, qr * W:(qr + 1) * W, m * V7X_LANES:(m + 1) * V7X_LANES] = tile(
                    int(dr[qr, 2 * m]), int(dr[qr, 2 * m + 1]))


def _na_kernel(q_ref, k_ref, v_ref, kc_ref, vc_ref, rp_ref, colneg_ref, o_ref, bias_ref, *, starts, pat_of, pats, kr,
               win_c):
    @pl.when(pl.program_id(1) == 0)
    def _():
        _na_build_bias(rp_ref, colneg_ref, bias_ref, pats, win_c)

    gq = NA_GROUP_ROWS * GRID_W
    kc = kc_ref[0, 0].astype(BF16)
    vc = vc_ref[0, 0].astype(BF16)
    scores = []
    for g, (ks, pid) in enumerate(zip(starts, pat_of)):
        q = q_ref[0, 0, g * gq:(g + 1) * gq]
        kw = k_ref[0, 0, ks * GRID_W:(ks + kr) * GRID_W]
        scores.append([_dot_nt(q, kw) + bias_ref[pid], _dot_nt(q, kc)])
    sm = [_softmax_parts(s) for s in scores]
    for g, (ks, (es, l)) in enumerate(zip(starts, sm)):
        vw = v_ref[0, 0, ks * GRID_W:(ks + kr) * GRID_W]
        o = (_dot(es[0].astype(BF16), vw) + _dot(es[1].astype(BF16), vc)) / l
        o_ref[0, g * gq:(g + 1) * gq, :] = o.astype(o_ref.dtype)


def _na_attention(qkv, k_ctx, v_ctx, rpb):
    B, H3, N, d = qkv.shape
    H = H3 // 3
    P = k_ctx.shape[2]
    rows = N // GRID_W
    win_r, win_c = (rpb.shape[1] + 1) // 2, (rpb.shape[2] + 1) // 2
    kr, starts, pat_of, pats, colneg = _na_tables(rows, win_r, win_c)
    rp = jnp.pad(rpb.astype(F32), ((0, 0), (0, -rpb.shape[1] % 8), (0, V7X_LANES - rpb.shape[2])))
    gq, gk = NA_GROUP_ROWS * GRID_W, kr * GRID_W
    need = (2 * (3 * N * d * 2 + 2 * P * d * 4 + N * d * 2) + len(pats) * gq * gk * 4
            + 3 * len(starts) * gq * (gk + P) * 4)
    return pl.pallas_call(
        functools.partial(_na_kernel, starts=starts, pat_of=pat_of, pats=pats, kr=kr, win_c=win_c),
        grid=(H, B),
        in_specs=[pl.BlockSpec((1, 1, N, d), lambda h, b: (b, h, 0, 0)),
                  pl.BlockSpec((1, 1, N, d), lambda h, b: (b, H + h, 0, 0)),
                  pl.BlockSpec((1, 1, N, d), lambda h, b: (b, 2 * H + h, 0, 0)),
                  pl.BlockSpec((1, 1, P, d), lambda h, b: (b, h, 0, 0)),
                  pl.BlockSpec((1, 1, P, d), lambda h, b: (b, h, 0, 0)),
                  pl.BlockSpec((1,) + rp.shape[1:], lambda h, b: (h, 0, 0)),
                  pl.BlockSpec(colneg.shape, lambda h, b: (0, 0))],
        out_specs=pl.BlockSpec((1, N, d), lambda h, b: (b, 0, h)),
        out_shape=jax.ShapeDtypeStruct((B, N, H * d), BF16),
        scratch_shapes=[pltpu.VMEM((len(pats), gq, gk), F32)],
        compiler_params=_cparams(("parallel", "arbitrary"), need),
        name="neighbourhood_attention",
    )(qkv, qkv, qkv, k_ctx, v_ctx, rp, jnp.asarray(colneg))


def _lat_diff_kernel(lam_ref, g_ref, q_ref, k_ref, v_ref, kc_ref, vc_ref, o_ref, *, lam_init):
    dq = lam_ref.shape[-1]
    tq = q_ref.shape[2]
    sub = min(tq, ATTN_ROWS)
    lam = _diff_lambda(lam_ref[0], lam_init)
    v = v_ref[0, 0]
    vc = vc_ref[0, 0].astype(BF16)
    scores = []
    for t in range(tq // sub):
        for c in range(2):
            cols = slice(c * dq, (c + 1) * dq)
            q = q_ref[0, 0, t * sub:(t + 1) * sub, cols]
            scores.append([_dot_nt(q, k_ref[0, 0, :, cols]), _dot_nt(q, kc_ref[0, 0, :, cols].astype(BF16))])
    sm = [_softmax_parts(s) for s in scores]
    os = [(_dot(es[0].astype(BF16), v) + _dot(es[1].astype(BF16), vc)) / l for es, l in sm]
    for t in range(tq // sub):
        o = os[2 * t] - lam * os[2 * t + 1]
        o_ref[0, t * sub:(t + 1) * sub, :] = _head_norm(o, g_ref[...], lam_init).astype(o_ref.dtype)


def _lat_diff_attention(qkv, k_ctx, v_ctx, lam_params, subln, idx, lam_init):
    B, H3, N, dv = qkv.shape
    H = H3 // 3
    P = k_ctx.shape[2]
    dq = lam_params.shape[-1]
    tq = _tile(N, 4 * ATTN_ROWS, 8)
    need = 2 * (tq * dv * 4 + 2 * N * dv * 2 + 2 * P * dv * 4) + 8 * tq * (N + P) * 4
    return pl.pallas_call(
        functools.partial(_lat_diff_kernel, lam_init=lam_init),
        grid=(B, H, N // tq),
        in_specs=[pl.BlockSpec((1, 4, dq), lambda b, h, t: (idx, 0, 0)),
                  pl.BlockSpec((None, 1, dv), lambda b, h, t: (idx, 0, 0)),
                  pl.BlockSpec((1, 1, tq, dv), lambda b, h, t: (b, h, t, 0)),
                  pl.BlockSpec((1, 1, N, dv), lambda b, h, t: (b, H + h, 0, 0)),
                  pl.BlockSpec((1, 1, N, dv), lambda b, h, t: (b, 2 * H + h, 0, 0)),
                  pl.BlockSpec((1, 1, P, dv), lambda b, h, t: (b, h, 0, 0)),
                  pl.BlockSpec((1, 1, P, dv), lambda b, h, t: (b, h, 0, 0))],
        out_specs=pl.BlockSpec((1, tq, dv), lambda b, h, t: (b, t, h)),
        out_shape=jax.ShapeDtypeStruct((B, N, H * dv), BF16),
        compiler_params=_cparams(("parallel", "parallel", "arbitrary"), need),
        name="latent_diff_attention",
    )(lam_params, subln, qkv, qkv, qkv, k_ctx, v_ctx)


def _rope_tables(n_lat, dq):
    half = dq // 2
    nf = half // 2
    t = jnp.arange(n_lat)
    pos_r = (t // GRID_W).astype(F32)
    pos_c = (t % GRID_W).astype(F32)
    inv = ROPE_THETA ** (-jnp.arange(nf, dtype=F32) / nf)
    zeros = jnp.zeros((n_lat, nf), F32)

    def tabs(pos):
        ang = pos[:, None] * inv[None, :]
        cos, sin = jnp.cos(ang), jnp.sin(ang)
        return (jnp.concatenate([cos, cos], -1), jnp.concatenate([-sin, zeros], -1),
                jnp.concatenate([zeros, sin], -1))

    r, c = tabs(pos_r), tabs(pos_c)
    return tuple(jnp.concatenate([a, b], -1) for a, b in zip(r, c))


def kernel(x_prompt, x_sample, cache_k_0, cache_v_0, cache_k_1, cache_v_1, cache_k_2, cache_v_2, cache_k_3, cache_v_3, c, c_ctx, norm_attn, norm_ffn, w_ada, b_ada, w_qkv, w_o, na_rpb, diff_lambda, diff_subln, w_up, conv_w, conv_b, w_down, final_norm):
    caches = [(cache_k_0, cache_v_0), (cache_k_1, cache_v_1), (cache_k_2, cache_v_2), (cache_k_3, cache_v_3)]
    Bp, Sp, D = x_prompt.shape
    Bs, Ns, _ = x_sample.shape
    L = w_qkv.shape[0]
    Hn = na_rpb.shape[1]
    dn = D // Hn
    dq = diff_lambda.shape[-1]
    dv = 2 * dq
    Hd = D // dv
    assert dq == V7X_LANES and dn % V7X_LANES == 0

    n_cond = 1 + Bs
    cond = jnp.concatenate([c_ctx[None, :], c, jnp.zeros((-n_cond % 16, D), F32)], axis=0)
    mod = _ada_params(cond, w_ada, b_ada).reshape(L, cond.shape[0], 1, 6 * D)

    w_qkv_b, w_o_b = w_qkv.astype(BF16), w_o.astype(BF16)
    w_down_b = w_down.astype(BF16)
    conv_b3 = conv_b.reshape(L, 1, conv_b.shape[-1])
    diff_subln = diff_subln.reshape(diff_subln.shape[0], 1, dv)
    rope_tabs = _rope_tables(Ns, dq)

    ctx_row = lambda b: 0
    ctx_tile_row = lambda i, bm: 0
    lat_row = lambda b: 1 + b
    lat_tile_row = lambda i, bm: 1 + (i * bm) // Ns

    xp = x_prompt.reshape(Bp * Sp, D)
    xs = x_sample.reshape(Bs * Ns, D)
    new_state = []
    for l in range(L):
        is_na = l % 2 == 0
        hw = dn if is_na else dv
        H = Hn if is_na else Hd
        q_scale = (dn if is_na else dq) ** -0.5
        lam_init = 0.8 - 0.6 * math.exp(-0.3 * l)

        h = _rms_norm(xp.reshape(Bp, Sp, D), norm_attn[l], BF16, mod, l, (0, 1), ctx_row).reshape(Bp * Sp, D)
        q = _mm_heads(h, w_qkv_b, l, 0, 1, Bp, hw, BF16, q_scale)
        k_new = _mm_heads(h, w_qkv_b, l, 1, 1, Bp, hw, F32, q_scale)
        v_new = _mm_heads(h, w_qkv_b, l, 2, 1, Bp, hw, F32, q_scale)
        if is_na:
            o = _ctx_dense_attention(q, k_new, v_new)
            new_state += [k_new, v_new]
        else:
            o = _ctx_diff_attention(q, k_new, v_new, diff_lambda, diff_subln, l // 2, lam_init)
            new_state += [k_new.reshape(Bp, H, Sp, 2, dq), v_new]
        xp = _mm_resid(o.reshape(Bp * Sp, D), w_o_b, l, xp, mod, 2, ctx_tile_row, 1024, 1024)
        h = _rms_norm(xp.reshape(Bp, Sp, D), norm_ffn[l], BF16, mod, l, (3, 4), ctx_row).reshape(Bp * Sp, D)
        u, w_up_halves = _upconv(h, w_up, conv_w, conv_b3, l, Sp)
        xp = _mm_resid(u, w_down_b, l, xp, mod, 5, ctx_tile_row, 512, 512)

        k_ctx, v_ctx = caches[l]
        h = _rms_norm(xs.reshape(Bs, Ns, D), norm_attn[l], BF16, mod, l, (0, 1), lat_row).reshape(Bs * Ns, D)
        if is_na:
            qkv = _mm_heads(h, w_qkv_b, l, 0, 3, Bs, hw, BF16, q_scale)
            o = _na_attention(qkv, k_ctx, v_ctx, na_rpb[l // 2])
        else:
            qkv = _mm_heads(h, w_qkv_b, l, 0, 3, Bs, hw, BF16, q_scale, rope_tabs)
            k_ctx = k_ctx.reshape(k_ctx.shape[0], H, k_ctx.shape[2], dv)
            o = _lat_diff_attention(qkv, k_ctx, v_ctx, diff_lambda, diff_subln, l // 2, lam_init)
        xs = _mm_resid(o.reshape(Bs * Ns, D), w_o_b, l, xs, mod, 2, lat_tile_row, 1024, 1024)
        h = _rms_norm(xs.reshape(Bs, Ns, D), norm_ffn[l], BF16, mod, l, (3, 4), lat_row).reshape(Bs * Ns, D)
        u = _upconv(h, None, conv_w, conv_b3, l, Ns, w_up_halves)
        xs = _mm_resid(u, w_down_b, l, xs, mod, 5, lat_tile_row, 512, 512)

    y_prompt = _rms_norm(xp.reshape(Bp, Sp, D), final_norm, F32)
    y_sample = _rms_norm(xs.reshape(Bs, Ns, D), final_norm, F32)
    return (y_prompt, y_sample, *new_state)
```
